```python
import jax
import jax.numpy as jnp
from jax import lax
import numpy as np

D_MODEL = 1024
BATCH = 2
SEQ = 8192
DEPTH = 2

N_META = 16
NORM_EPS = 1e-6
SSM_D_INNER = 2 * D_MODEL
SSM_HEAD_DIM = 64
SSM_HEADS = SSM_D_INNER // SSM_HEAD_DIM
SSM_GROUPS = 4
SSM_HEADS_PER_GROUP = SSM_HEADS // SSM_GROUPS
SSM_STATE = 128
SSM_CONV = 4
SSM_CHUNK = 256
SSM_CONV_DIM = SSM_D_INNER + 2 * SSM_GROUPS * SSM_STATE
SSM_IN_DIM = SSM_D_INNER + SSM_CONV_DIM + SSM_HEADS
SB_HEAD_DIM = 64
SB_HEADS = D_MODEL // SB_HEAD_DIM
SB_WIDTH = SB_HEADS * SB_HEAD_DIM
SB_Q_BLOCK = 128
D_FF = 256 * ((8 * D_MODEL // 3 + 255) // 256)
FFN_CONV = 3

kernel_name = 'hybrid_ssd_stickbreaking_yoco'


def _rmsnorm(x, g):
    x32 = x.astype(jnp.float32)
    y = x32 * lax.rsqrt(jnp.mean(x32 * x32, axis=-1, keepdims=True) + NORM_EPS)
    return (y * g.astype(jnp.float32)).astype(x.dtype)


def _causal_dwconv(x, w, bias):
    width = w.shape[0]
    L = x.shape[1]
    xp = jnp.pad(x, ((0, 0), (width - 1, 0), (0, 0)))
    y = xp[:, 0:L] * w[0] + bias
    for k in range(1, width):
        y = y + xp[:, k:k + L] * w[k]
    return y


def _ssd_mixer(u, w_in, conv_w, conv_b, dt_bias, a_log, d_skip, gate_g, w_out):
    b, L, _ = u.shape
    G, E, P, N, Q = SSM_GROUPS, SSM_HEADS_PER_GROUP, SSM_HEAD_DIM, SSM_STATE, SSM_CHUNK
    f32 = jnp.float32
    z, xbc, dt_raw = jnp.split(u @ w_in, [SSM_D_INNER, SSM_D_INNER + SSM_CONV_DIM], axis=-1)
    xbc = jax.nn.silu(_causal_dwconv(xbc, conv_w, conv_b))
    xs, b_in, c_in = jnp.split(xbc, [SSM_D_INNER, SSM_D_INNER + G * N], axis=-1)
    dt = jax.nn.softplus(dt_raw.astype(f32) + dt_bias.astype(f32))
    a = -jnp.exp(a_log.astype(f32))
    pf = (-N_META) % Q
    pe = (-(pf + L)) % Q
    nc = (pf + L + pe) // Q

    def to_chunks(t, tail):
        return jnp.pad(t, ((0, 0), (pf, pe), (0, 0))).reshape((b, nc, Q) + tail)

    x_c = to_chunks(xs, (G, E, P)).astype(f32)
    b_c = to_chunks(b_in, (G, N)).astype(f32)
    c_c = to_chunks(c_in, (G, N)).astype(f32)
    dt_c = to_chunks(dt, (G, E))
    xdt = x_c * dt_c[..., None]
    a_cs = jnp.cumsum(jnp.transpose(dt_c * a.reshape(G, E), (0, 3, 4, 1, 2)), axis=-1)
    causal = jnp.tril(jnp.ones((Q, Q), dtype=bool))
    decay_in = jnp.exp(jnp.where(causal, a_cs[..., :, None] - a_cs[..., None, :], -jnp.inf))
    cb = jnp.einsum('bclgn,bcsgn->bgcls', c_c, b_c)
    y_diag = jnp.einsum('bgcls,bgecls,bcsgep->bclgep', cb, decay_in, xdt)
    decay_to_end = jnp.exp(a_cs[..., -1:] - a_cs)
    chunk_states = jnp.einsum('bclgn,bgecl,bclgep->cbgepn', b_c, decay_to_end, xdt)
    chunk_decay = jnp.moveaxis(jnp.exp(a_cs[..., -1]), -1, 0)

    def step(state, inp):
        s_new, d = inp
        return state * d[..., None, None] + s_new, state

    _, prev_states = lax.scan(step, jnp.zeros((b, G, E, P, N), f32), (chunk_states, chunk_decay))
    y_off = jnp.einsum('bclgn,cbgepn,bgecl->bclgep', c_c, prev_states, jnp.exp(a_cs))
    y = (y_diag + y_off).reshape(b, nc * Q, SSM_D_INNER)[:, pf:pf + L]
    y = y + (xs.reshape(b, L, SSM_HEADS, P).astype(f32) * d_skip.astype(f32)[:, None]).reshape(b, L, SSM_D_INNER)
    hg = (y * jax.nn.silu(z.astype(f32))).reshape(b, L, G, SSM_D_INNER // G)
    hg = hg * lax.rsqrt(jnp.mean(hg * hg, axis=-1, keepdims=True) + NORM_EPS)
    hg = hg.reshape(b, L, SSM_D_INNER) * gate_g.astype(f32)
    return hg.astype(u.dtype) @ w_out


def _stick_breaking_attention(q, k, v):
    b, L, H, Dh = q.shape
    lp = -(-L // SB_Q_BLOCK) * SB_Q_BLOCK
    pad = ((0, 0), (0, lp - L), (0, 0), (0, 0))
    q, k, v = jnp.pad(q, pad), jnp.pad(k, pad), jnp.pad(v, pad)
    scale = Dh ** -0.5
    outs = []
    for i in range(lp // SB_Q_BLOCK):
        t0, t1 = i * SB_Q_BLOCK, (i + 1) * SB_Q_BLOCK
        logits = jnp.einsum('bthd,bshd->bhts', q[:, t0:t1], k[:, :t1]).astype(jnp.float32) * scale
        t_idx = t0 + jnp.arange(SB_Q_BLOCK)[:, None]
        s_idx = jnp.arange(t1)[None, :]
        visible = s_idx < t_idx
        log_keep = jnp.where(visible, jax.nn.log_sigmoid(-logits), 0.0)
        later = lax.cumsum(log_keep, axis=3, reverse=True) - log_keep
        log_w = jnp.where(visible, jax.nn.log_sigmoid(logits) + later, -jnp.inf)
        w = jnp.exp(log_w).astype(v.dtype)
        outs.append(jnp.einsum('bhts,bshd->bthd', w, v[:, :t1]))
    return jnp.concatenate(outs, axis=1)[:, :L]


def _conv_ffn(u, w_up, conv_w, conv_b, w_down):
    h = _causal_dwconv(u @ w_up, conv_w, conv_b)
    g, val = jnp.split(h, 2, axis=-1)
    return (jax.nn.silu(g) * val) @ w_down


def setup_inputs(seed: int = 0) -> dict:
    key = jax.random.key(seed)
    ks = jax.random.split(key, 32)
    n_a = DEPTH // 2
    n_b = DEPTH - n_a
    f32 = jnp.float32

    def nrm(k, shape, scale):
        return jax.random.normal(k, shape, f32) * scale

    def gain(k, shape):
        return 1.0 + 0.02 * jax.random.normal(k, shape, f32)

    dt0 = jnp.exp(jax.random.uniform(ks[6], (n_a, SSM_HEADS), f32, np.log(1e-3), np.log(1e-1)))
    dt_bias = dt0 + jnp.log(-jnp.expm1(-dt0))
    return {
        'x': jax.random.normal(ks[0], (BATCH, SEQ, D_MODEL), f32),
        'meta_tokens': nrm(ks[1], (N_META, D_MODEL), 1.0),
        'ssd_norm': gain(ks[2], (n_a, D_MODEL)),
        'ssd_w_in': nrm(ks[3], (n_a, D_MODEL, SSM_IN_DIM), D_MODEL ** -0.5),
        'ssd_conv_w': nrm(ks[4], (n_a, SSM_CONV, SSM_CONV_DIM), SSM_CONV ** -0.5),
        'ssd_conv_b': nrm(ks[5], (n_a, SSM_CONV_DIM), 0.02),
        'ssd_dt_bias': dt_bias,
        'ssd_a_log': jnp.log(jax.random.uniform(ks[7], (n_a, SSM_HEADS), f32, 1.0, 16.0)),
        'ssd_d_skip': jax.random.uniform(ks[8], (n_a, SSM_HEADS), f32, 0.5, 1.5),
        'ssd_gate_norm': gain(ks[9], (n_a, SSM_D_INNER)),
        'ssd_w_out': nrm(ks[10], (n_a, SSM_D_INNER, D_MODEL), SSM_D_INNER ** -0.5),
        'kv_norm': gain(ks[11], (D_MODEL,)),
        'w_kv': nrm(ks[12], (D_MODEL, 2 * SB_WIDTH), D_MODEL ** -0.5),
        'sb_norm': gain(ks[13], (n_b, D_MODEL)),
        'sb_w_q': nrm(ks[14], (n_b, D_MODEL, SB_WIDTH), D_MODEL ** -0.5),
        'sb_w_o': nrm(ks[15], (n_b, SB_WIDTH, D_MODEL), SB_WIDTH ** -0.5),
        'ffn_norm': gain(ks[16], (DEPTH, D_MODEL)),
        'ffn_w_up': nrm(ks[17], (DEPTH, D_MODEL, 2 * D_FF), D_MODEL ** -0.5),
        'ffn_conv_w': nrm(ks[18], (DEPTH, FFN_CONV, 2 * D_FF), FFN_CONV ** -0.5),
        'ffn_conv_b': nrm(ks[19], (DEPTH, 2 * D_FF), 0.02),
        'ffn_w_down': nrm(ks[20], (DEPTH, D_FF, D_MODEL), D_FF ** -0.5),
        'final_norm': gain(ks[21], (D_MODEL,)),
    }


def reference(x, meta_tokens, ssd_norm, ssd_w_in, ssd_conv_w, ssd_conv_b, ssd_dt_bias, ssd_a_log,
              ssd_d_skip, ssd_gate_norm, ssd_w_out, kv_norm, w_kv, sb_norm, sb_w_q, sb_w_o,
              ffn_norm, ffn_w_up, ffn_conv_w, ffn_conv_b, ffn_w_down, final_norm):
    b = x.shape[0]
    n_a = DEPTH // 2
    h = jnp.concatenate([jnp.broadcast_to(meta_tokens[None], (b, N_META, D_MODEL)).astype(x.dtype), x], axis=1)
    L = h.shape[1]
    k_shared = None
    v_shared = None
    for layer in range(DEPTH):
        if layer < n_a:
            h = h + _ssd_mixer(_rmsnorm(h, ssd_norm[layer]), ssd_w_in[layer], ssd_conv_w[layer],
                               ssd_conv_b[layer], ssd_dt_bias[layer], ssd_a_log[layer],
                               ssd_d_skip[layer], ssd_gate_norm[layer], ssd_w_out[layer])
        else:
            if layer == n_a:
                kv = _rmsnorm(h, kv_norm) @ w_kv
                k_shared, v_shared = jnp.split(kv.reshape(b, L, 2, SB_HEADS, SB_HEAD_DIM), 2, axis=2)
                k_shared, v_shared = k_shared[:, :, 0], v_shared[:, :, 0]
            j = layer - n_a
            q = (_rmsnorm(h, sb_norm[j]) @ sb_w_q[j]).reshape(b, L, SB_HEADS, SB_HEAD_DIM)
            o = _stick_breaking_attention(q, k_shared, v_shared).reshape(b, L, SB_WIDTH)
            h = h + o @ sb_w_o[j]
        h = h + _conv_ffn(_rmsnorm(h, ffn_norm[layer]), ffn_w_up[layer], ffn_conv_w[layer],
                          ffn_conv_b[layer], ffn_w_down[layer])
    return _rmsnorm(h, final_norm)[:, N_META:]
```

```python
import functools

import jax
import jax.numpy as jnp
from jax import lax
from jax.experimental import pallas as pl
from jax.experimental.pallas import tpu as pltpu

F32 = jnp.float32
BF16 = jnp.bfloat16

N_META = 16
NORM_EPS = 1e-6
SSM_HEAD_DIM = 64
SSM_GROUPS = 4
SSM_STATE = 128
SSM_CHUNK = 256
SB_HEAD_DIM = 64

LANES = 128
SUBLANES = 8
ROW_TILE = 768
FFN_COL_TILE = 256
XBC_COL_TILE = 512
ATTN_TILE = 256
VMEM_LIMIT = 56 * 1024 * 1024


def _resident(shape):
  nd = len(shape)
  return pl.BlockSpec(shape, lambda *_: (0,) * nd, pipeline_mode=pl.Buffered(1))


def _dot(a, b):
  return jnp.dot(a, b, preferred_element_type=F32)


def _split3(x):
  hi = x.astype(BF16)
  r1 = x - hi.astype(F32)
  mid = r1.astype(BF16)
  lo = (r1 - mid.astype(F32)).astype(BF16)
  return hi, mid, lo


def _dot_exact_rhs(x, m):
  hi, mid, lo = _split3(x)
  return _dot(hi, m) + _dot(mid, m) + _dot(lo, m)


def _dot_exact_lhs(m, x):
  hi, mid, lo = _split3(x)
  return _dot(m, hi) + _dot(m, mid) + _dot(m, lo)


def _rmsnorm(x, g):
  ms = jnp.mean(x * x, axis=-1, keepdims=True)
  return x * lax.rsqrt(ms + NORM_EPS) * g


def _sigmoid(x):
  return 1.0 / (1.0 + jnp.exp(-x))


def _softplus(x):
  return jnp.maximum(x, 0.0) + jnp.log(1.0 + jnp.exp(-jnp.abs(x)))


def _causal_conv(p, prev, w, b):
  width = w.shape[0]
  cur = w[width - 1:width, :]
  y = p * cur + b
  head = jnp.concatenate([prev, p[0:SUBLANES, :]], axis=0)
  y_head = p[0:SUBLANES, :] * cur + b
  for k in range(1, width):
    tap = w[width - 1 - k:width - k, :]
    y = y + pltpu.roll(p, k, 0) * tap
    y_head = y_head + pltpu.roll(head, k, 0)[SUBLANES:2 * SUBLANES, :] * tap
  return jnp.concatenate([y_head, y[SUBLANES:, :]], axis=0)


def _ssd_in_kernel(h_ref, g_ref, wz_ref, wx_ref, wdt_ref, cw_ref, cb_ref,
                   dtb_ref, z_ref, xbc_ref, dt_ref, carry_ref, *, pf, n_heads):
  i = pl.program_id(1)
  tm = h_ref.shape[0]

  @pl.when(i == 0)
  def _():
    carry_ref[...] = jnp.zeros_like(carry_ref)

  u = _rmsnorm(h_ref[...], g_ref[...]).astype(BF16)
  z_ref[...] = _dot(u, wz_ref[...]).astype(BF16)

  dt = _softplus(_dot(u, wdt_ref[...]) + dtb_ref[...])
  row = i * tm + lax.broadcasted_iota(jnp.int32, dt.shape, 0)
  lane = lax.broadcasted_iota(jnp.int32, dt.shape, 1)
  dt_ref[...] = jnp.where((row >= pf) & (lane < n_heads), dt, 0.0)

  n_ch = xbc_ref.shape[1]
  for c in range(n_ch // XBC_COL_TILE):
    cs = slice(c * XBC_COL_TILE, (c + 1) * XBC_COL_TILE)
    p = _dot(u, wx_ref[:, cs])
    y = _causal_conv(p, carry_ref[:, cs], cw_ref[:, cs], cb_ref[:, cs])
    carry_ref[:, cs] = p[tm - SUBLANES:tm, :]
    xbc_ref[:, cs] = (y * _sigmoid(y)).astype(BF16)


def _ssd_in(h, g, wz, wx, wdt, cw, cb, dtb, *, batch, lp, pf, n_heads):
  d = h.shape[1]
  n_z, n_x, n_dt = wz.shape[1], wx.shape[1], wdt.shape[1]
  nt = lp // ROW_TILE
  rows = lambda n: pl.BlockSpec((ROW_TILE, n), lambda b, i: (b * nt + i, 0))
  return pl.pallas_call(
      functools.partial(_ssd_in_kernel, pf=pf, n_heads=n_heads),
      grid=(batch, nt),
      in_specs=[rows(d), _resident((1, d)), _resident(wz.shape),
                _resident(wx.shape), _resident(wdt.shape), _resident(cw.shape),
                _resident(cb.shape), _resident(dtb.shape)],
      out_specs=[rows(n_z), rows(n_x), rows(n_dt)],
      out_shape=[jax.ShapeDtypeStruct((batch * lp, n_z), BF16),
                 jax.ShapeDtypeStruct((batch * lp, n_x), BF16),
                 jax.ShapeDtypeStruct((batch * lp, n_dt), F32)],
      scratch_shapes=[pltpu.VMEM((SUBLANES, n_x), F32)],
      compiler_params=pltpu.CompilerParams(
          dimension_semantics=("arbitrary", "arbitrary"),
          vmem_limit_bytes=VMEM_LIMIT),
      name="ssd_in",
  )(h, g, wz, wx, wdt, cw, cb, dtb)


def _ssd_kernel(xbc_ref, bt_ref, z_ref, dt_ref, dtt_ref, h_ref, alog_row_ref,
                alog_col_ref, dskip_ref, gg_ref, wout_ref, expand_ref, o_ref,
                state_ref, *, d_inner):
  c = pl.program_id(1)
  q = h_ref.shape[0]
  n_state = SSM_STATE
  g_width = d_inner // SSM_GROUPS
  pairs_per_group = g_width // LANES

  @pl.when(c == 0)
  def _():
    state_ref[...] = jnp.zeros_like(state_ref)

  ri = lax.broadcasted_iota(jnp.int32, (q, q), 0)
  ci = lax.broadcasted_iota(jnp.int32, (q, q), 1)
  causal = ri >= ci
  lower = causal.astype(BF16)
  upper = (ri <= ci).astype(BF16)

  dt = dt_ref[...]
  acs = _dot_exact_lhs(lower, dt * -jnp.exp(alog_row_ref[...]))
  acs_t = _dot_exact_rhs(dtt_ref[...] * -jnp.exp(alog_col_ref[...]), upper)
  a_last = acs[q - 1:q, :]
  decay_to_end = jnp.exp(a_last - acs)
  decay_from_start = jnp.exp(acs)

  expand = expand_ref[...]
  dt_x = _dot(dt.astype(BF16), expand)
  dte_x = _dot(decay_to_end.astype(BF16), expand)
  dfs_x = _dot(decay_from_start.astype(BF16), expand)
  chunk_decay_x = _dot_exact_rhs(
      decay_from_start[q - SUBLANES:q, :], expand)[SUBLANES - 1:SUBLANES, :]

  xs = xbc_ref[:, 0:d_inner].astype(F32)
  xdt = xs * dt_x
  xdt_end = (xdt * dte_x).astype(BF16)
  xdt_b = xdt.astype(BF16)
  even_head = (lax.broadcasted_iota(jnp.int32, (1, d_inner), 1) % LANES
               ) < SSM_HEAD_DIM
  zero = jnp.zeros_like(xdt_b)
  xdt_even = jnp.where(even_head, xdt_b, zero)
  xdt_odd = jnp.where(even_head, zero, xdt_b)

  c_off = d_inner + SSM_GROUPS * n_state
  y_parts = []
  for g in range(SSM_GROUPS):
    gs = slice(g * g_width, (g + 1) * g_width)
    c_g = xbc_ref[:, c_off + g * n_state:c_off + (g + 1) * n_state]
    bt_g = bt_ref[g * n_state:(g + 1) * n_state, :]
    cb = _dot(c_g, bt_g)
    s_prev = state_ref[g]
    y_off = _dot(c_g, s_prev.astype(BF16)) * dfs_x[:, gs]
    state_ref[g] = s_prev * chunk_decay_x[:, gs] + _dot(bt_g, xdt_end[:, gs])
    pair_out = []
    for pr in range(pairs_per_group):
      cs = slice(g * g_width + pr * LANES, g * g_width + (pr + 1) * LANES)
      acc = None
      for half, xsrc in ((0, xdt_even), (1, xdt_odd)):
        hd = (g * g_width + pr * LANES) // SSM_HEAD_DIM + half
        diff = acs[:, hd:hd + 1] - acs_t[hd:hd + 1, :]
        decay = jnp.exp(jnp.where(causal, diff, -jnp.inf))
        part = _dot((cb * decay).astype(BF16), xsrc[:, cs])
        acc = part if acc is None else acc + part
      pair_out.append(acc)
    y_parts.append(jnp.concatenate(pair_out, axis=1) + y_off)

  zf = z_ref[...].astype(F32)
  gate = zf * _sigmoid(zf)
  normed = []
  for g in range(SSM_GROUPS):
    gs = slice(g * g_width, (g + 1) * g_width)
    hg = (y_parts[g] + xs[:, gs] * dskip_ref[:, gs]) * gate[:, gs]
    normed.append(_rmsnorm(hg, gg_ref[:, gs]).astype(BF16))
  hg_all = jnp.concatenate(normed, axis=1)
  o_ref[...] = h_ref[...] + _dot(hg_all, wout_ref[...])


def _ssd(xbc, bt, z, dt, dtt, h, alog_row, alog_col, dskip_x, gate_g, wout,
         expand, *, batch, lp):
  d = h.shape[1]
  d_inner = z.shape[1]
  q = SSM_CHUNK
  nc = lp // q
  n_heads = dtt.shape[1]
  rows = lambda n: pl.BlockSpec((q, n), lambda b, c: (b * nc + c, 0))
  cols = lambda n: pl.BlockSpec((None, n, q), lambda b, c: (b, 0, c))
  return pl.pallas_call(
      functools.partial(_ssd_kernel, d_inner=d_inner),
      grid=(batch, nc),
      in_specs=[rows(xbc.shape[1]), cols(bt.shape[1]), rows(d_inner),
                rows(dt.shape[1]), cols(n_heads), rows(d),
                _resident(alog_row.shape), _resident(alog_col.shape),
                _resident(dskip_x.shape), _resident(gate_g.shape),
                _resident(wout.shape), _resident(expand.shape)],
      out_specs=rows(d),
      out_shape=jax.ShapeDtypeStruct((batch * lp, d), F32),
      scratch_shapes=[pltpu.VMEM(
          (SSM_GROUPS, SSM_STATE, d_inner // SSM_GROUPS), F32)],
      compiler_params=pltpu.CompilerParams(
          dimension_semantics=("arbitrary", "arbitrary"),
          vmem_limit_bytes=VMEM_LIMIT),
      name="ssd_scan",
  )(xbc, bt, z, dt, dtt, h, alog_row, alog_col, dskip_x, gate_g, wout, expand)


def _ffn_kernel(*refs, pf, d_ff, has_attn, has_final):
  refs = list(refs)
  h_ref = refs.pop(0)
  if has_attn:
    attn_ref, wo_ref = refs.pop(0), refs.pop(0)
  g_ref, wup_ref, cw_ref, cb_ref, wdn_ref = (refs.pop(0) for _ in range(5))
  if has_final:
    fin_ref = refs.pop(0)
  o_ref, carry_ref = refs
  i = pl.program_id(1)
  tm = h_ref.shape[0]

  @pl.when(i == 0)
  def _():
    carry_ref[...] = jnp.zeros_like(carry_ref)

  h = h_ref[...]
  if has_attn:
    h = h + _dot(attn_ref[...], wo_ref[...])
  u = _rmsnorm(h, g_ref[...]).astype(BF16)
  acc = jnp.zeros(h.shape, F32)
  for c in range(d_ff // FFN_COL_TILE):
    gs = slice(c * FFN_COL_TILE, (c + 1) * FFN_COL_TILE)
    vs = slice(d_ff + c * FFN_COL_TILE, d_ff + (c + 1) * FFN_COL_TILE)
    pg = _dot(u, wup_ref[:, gs])
    pv = _dot(u, wup_ref[:, vs])
    yg = _causal_conv(pg, carry_ref[:, gs], cw_ref[:, gs], cb_ref[:, gs])
    yv = _causal_conv(pv, carry_ref[:, vs], cw_ref[:, vs], cb_ref[:, vs])
    carry_ref[:, gs] = pg[tm - SUBLANES:tm, :]
    carry_ref[:, vs] = pv[tm - SUBLANES:tm, :]
    act = (yg * _sigmoid(yg) * yv).astype(BF16)
    acc = acc + _dot(act, wdn_ref[gs, :])
  out = h + acc
  row = i * tm + lax.broadcasted_iota(jnp.int32, out.shape, 0)
  out = jnp.where(row >= pf, out, 0.0)
  if has_final:
    out = _rmsnorm(out, fin_ref[...])
  o_ref[...] = out


def _ffn(h, attn, wo, g, wup, cw, cb, wdn, fin_g, *, batch, lp, pf):
  d = h.shape[1]
  d_ff = wdn.shape[0]
  nt = lp // ROW_TILE
  rows = lambda n: pl.BlockSpec((ROW_TILE, n), lambda b, i: (b * nt + i, 0))
  has_attn = attn is not None
  has_final = fin_g is not None
  args, specs = [h], [rows(d)]
  if has_attn:
    args += [attn, wo]
    specs += [rows(attn.shape[1]), _resident(wo.shape)]
  args += [g, wup, cw, cb, wdn]
  specs += [_resident(a.shape) for a in (g, wup, cw, cb, wdn)]
  if has_final:
    args.append(fin_g)
    specs.append(_resident(fin_g.shape))
  return pl.pallas_call(
      functools.partial(_ffn_kernel, pf=pf, d_ff=d_ff, has_attn=has_attn,
                        has_final=has_final),
      grid=(batch, nt),
      in_specs=specs,
      out_specs=rows(d),
      out_shape=jax.ShapeDtypeStruct((batch * lp, d), F32),
      scratch_shapes=[pltpu.VMEM((SUBLANES, 2 * d_ff), F32)],
      compiler_params=pltpu.CompilerParams(
          dimension_semantics=("arbitrary", "arbitrary"),
          vmem_limit_bytes=VMEM_LIMIT),
      name="conv_ffn",
  )(*args)


def _qkv_kernel(h_ref, gkv_ref, gq_ref, wkv_ref, wq_ref, kv_ref, q_ref, *,
                q_scale):
  h = h_ref[...]
  hn = h * lax.rsqrt(jnp.mean(h * h, axis=-1, keepdims=True) + NORM_EPS)
  kv_ref[...] = _dot((hn * gkv_ref[...]).astype(BF16), wkv_ref[...]).astype(BF16)
  q = _dot((hn * gq_ref[...]).astype(BF16), wq_ref[...])
  q_ref[...] = (q * q_scale).astype(BF16)


def _qkv(h, gkv, gq, wkv, wq, *, q_scale):
  n, d = h.shape
  rows = lambda m: pl.BlockSpec((ROW_TILE, m), lambda i: (i, 0))
  return pl.pallas_call(
      functools.partial(_qkv_kernel, q_scale=q_scale),
      grid=(n // ROW_TILE,),
      in_specs=[rows(d), _resident(gkv.shape), _resident(gq.shape),
                _resident(wkv.shape), _resident(wq.shape)],
      out_specs=[rows(wkv.shape[1]), rows(wq.shape[1])],
      out_shape=[jax.ShapeDtypeStruct((n, wkv.shape[1]), BF16),
                 jax.ShapeDtypeStruct((n, wq.shape[1]), BF16)],
      compiler_params=pltpu.CompilerParams(
          dimension_semantics=("arbitrary",), vmem_limit_bytes=VMEM_LIMIT),
      name="qkv_proj",
  )(h, gkv, gq, wkv, wq)


def _attn_kernel(q_ref, k_ref, v_ref, o_ref, acc_ref, r_ref, *, pf):
  i = pl.program_id(2)
  t = q_ref.shape[0]
  first_head = lax.broadcasted_iota(jnp.int32, (1, LANES), 1) < SB_HEAD_DIM
  q = q_ref[...]
  zq = jnp.zeros_like(q)
  q_heads = (jnp.where(first_head, q, zq), jnp.where(first_head, zq, q))
  ri = lax.broadcasted_iota(jnp.int32, (t, t), 0)
  ci = lax.broadcasted_iota(jnp.int32, (t, t), 1)
  suffix = (ri >= ci).astype(BF16)

  def block(j, masked, carry):
    acc, r_heads = carry[0], carry[1:]
    start = pl.multiple_of(j * t, t)
    kb = k_ref[pl.ds(start, t), :]
    vb = v_ref[pl.ds(start, t), :]
    zv = jnp.zeros_like(vb)
    v_heads = (jnp.where(first_head, vb, zv), jnp.where(first_head, zv, vb))
    if masked:
      s_idx = j * t + ci
      visible = (s_idx < i * t + ri) & (s_idx >= pf)
    new_r = []
    for qa, va, ra in zip(q_heads, v_heads, r_heads):
      x = lax.dot_general(qa, kb, (((1,), (1,)), ((), ())),
                          preferred_element_type=F32)
      log_keep = -_softplus(x)
      if masked:
        log_keep = jnp.where(visible, log_keep, 0.0)
      incl = _dot(log_keep.astype(BF16), suffix)
      arg = x + ra + incl
      if masked:
        arg = jnp.where(visible, arg, -jnp.inf)
      acc = acc + _dot(jnp.exp(arg).astype(BF16), va)
      new_r.append(ra + incl[:, 0:1])
    return (acc, *new_r)

  zero_r = jnp.zeros((t, 1), F32)
  carry = block(i, True, (jnp.zeros((t, LANES), F32), zero_r, zero_r))
  carry = lax.fori_loop(
      0, jnp.maximum(i - 1, 0),
      lambda jj, cr: block(i - 1 - jj, False, cr), carry)
  acc_ref[...] = carry[0]
  r_ref[0] = carry[1]
  r_ref[1] = carry[2]

  @pl.when(i >= 1)
  def _():
    out = block(0, True, (acc_ref[...], r_ref[0], r_ref[1]))
    acc_ref[...] = out[0]

  o_ref[...] = acc_ref[...].astype(BF16)


def _attention(q, kv, *, batch, lp, pf):
  n, width = q.shape
  t = ATTN_TILE
  nq = lp // t
  n_pairs = width // LANES
  return pl.pallas_call(
      functools.partial(_attn_kernel, pf=pf),
      grid=(batch, n_pairs, nq),
      in_specs=[
          pl.BlockSpec((t, LANES), lambda b, p, i: (b * nq + i, p)),
          pl.BlockSpec((lp, LANES), lambda b, p, i: (b, p)),
          pl.BlockSpec((lp, LANES), lambda b, p, i: (b, n_pairs + p)),
      ],
      out_specs=pl.BlockSpec((t, LANES), lambda b, p, i: (b * nq + i, p)),
      out_shape=jax.ShapeDtypeStruct((n, width), BF16),
      scratch_shapes=[pltpu.VMEM((t, LANES), F32), pltpu.VMEM((2, t, 1), F32)],
      compiler_params=pltpu.CompilerParams(
          dimension_semantics=("arbitrary", "arbitrary", "arbitrary"),
          vmem_limit_bytes=VMEM_LIMIT),
      name="sb_attention",
  )(q, kv, kv)


def _row(v, width=None):
  v = v.astype(F32).reshape(1, -1)
  if width is not None and v.shape[1] < width:
    v = jnp.pad(v, ((0, 0), (0, width - v.shape[1])))
  return v


def kernel(x, meta_tokens, ssd_norm, ssd_w_in, ssd_conv_w, ssd_conv_b, ssd_dt_bias, ssd_a_log, ssd_d_skip, ssd_gate_norm, ssd_w_out, kv_norm, w_kv, sb_norm, sb_w_q, sb_w_o, ffn_norm, ffn_w_up, ffn_conv_w, ffn_conv_b, ffn_w_down, final_norm):
  batch, seq, d = x.shape
  depth = ffn_norm.shape[0]
  n_a = ssd_norm.shape[0]
  d_inner = ssd_w_out.shape[1]
  n_heads = ssd_a_log.shape[1]
  conv_dim = ssd_conv_w.shape[2]
  n_meta = meta_tokens.shape[0]
  length = n_meta + seq
  pf = (-n_meta) % SSM_CHUNK
  lp = pf + length
  assert lp % SSM_CHUNK == 0 and lp % ROW_TILE == 0 and lp % ATTN_TILE == 0
  assert n_heads <= LANES and d_inner == n_heads * SSM_HEAD_DIM
  dims = dict(batch=batch, lp=lp)

  h = jnp.concatenate([
      jnp.zeros((batch, pf, d), F32),
      jnp.broadcast_to(meta_tokens[None].astype(F32), (batch, n_meta, d)),
      x.astype(F32)], axis=1).reshape(batch * lp, d)

  expand = (jnp.arange(d_inner)[None, :] // SSM_HEAD_DIM
            == jnp.arange(LANES)[:, None]).astype(BF16)

  kv = None
  for layer in range(depth):
    attn = wo = None
    if layer < n_a:
      w_in = ssd_w_in[layer].astype(BF16)
      wz = w_in[:, :d_inner]
      wx = w_in[:, d_inner:d_inner + conv_dim]
      wdt = jnp.pad(w_in[:, d_inner + conv_dim:], ((0, 0), (0, LANES - n_heads)))
      z, xbc, dt = _ssd_in(
          h, _row(ssd_norm[layer]), wz, wx, wdt, ssd_conv_w[layer].astype(F32),
          _row(ssd_conv_b[layer]), _row(ssd_dt_bias[layer], LANES),
          pf=pf, n_heads=n_heads, **dims)
      b_off = d_inner
      bt = jnp.swapaxes(
          xbc[:, b_off:b_off + SSM_GROUPS * SSM_STATE].reshape(batch, lp, -1), 1, 2)
      dtt = jnp.swapaxes(dt[:, :n_heads].reshape(batch, lp, n_heads), 1, 2)
      h = _ssd(xbc, bt, z, dt, dtt, h, _row(ssd_a_log[layer], LANES),
               ssd_a_log[layer].astype(F32).reshape(n_heads, 1),
               _row(jnp.repeat(ssd_d_skip[layer], SSM_HEAD_DIM)),
               _row(ssd_gate_norm[layer]), ssd_w_out[layer].astype(BF16),
               expand, **dims)
    else:
      j = layer - n_a
      if layer == n_a:
        kv, q = _qkv(h, _row(kv_norm), _row(sb_norm[j]), w_kv.astype(BF16),
                     sb_w_q[j].astype(BF16), q_scale=SB_HEAD_DIM ** -0.5)
      else:
        _, q = _qkv(h, _row(kv_norm), _row(sb_norm[j]), w_kv.astype(BF16),
                    sb_w_q[j].astype(BF16), q_scale=SB_HEAD_DIM ** -0.5)
      attn = _attention(q, kv, pf=pf, **dims)
      wo = sb_w_o[j].astype(BF16)
    fin_g = _row(final_norm) if layer == depth - 1 else None
    h = _ffn(h, attn, wo, _row(ffn_norm[layer]), ffn_w_up[layer].astype(BF16),
             ffn_conv_w[layer].astype(F32), _row(ffn_conv_b[layer]),
             ffn_w_down[layer].astype(BF16), fin_g, pf=pf, **dims)
  return h.reshape(batch, lp, d)[:, pf + n_meta:].astype(x.dtype)
```

```python
import functools

import jax
import jax.numpy as jnp
from jax import lax
from jax.experimental import pallas as pl
from jax.experimental.pallas import tpu as pltpu

F32 = jnp.float32
BF16 = jnp.bfloat16

N_META = 16
NORM_EPS = 1e-6
SSM_HEAD_DIM = 64
SSM_GROUPS = 4
SSM_STATE = 128
SSM_CHUNK = 256
SB_HEAD_DIM = 64
LOG2_E = 1.4426950408889634

LANES = 128
SUBLANES = 8
ROW_TILE = 768
FFN_COL_TILE = 256
XBC_COL_TILE = 512
ATTN_TILE = 256
ATTN_PAIRS = 2
ATTN_DEAD_LOG2 = -160.0
VMEM_LIMIT = 56 * 1024 * 1024


def _resident(shape):
  nd = len(shape)
  return pl.BlockSpec(shape, lambda *_: (0,) * nd, pipeline_mode=pl.Buffered(1))


def _dot(a, b):
  return jnp.dot(a, b, preferred_element_type=F32)


def _split3(x):
  hi = x.astype(BF16)
  r1 = x - hi.astype(F32)
  mid = r1.astype(BF16)
  lo = (r1 - mid.astype(F32)).astype(BF16)
  return hi, mid, lo


def _dot_exact_rhs(x, m):
  hi, mid, lo = _split3(x)
  return _dot(hi, m) + _dot(mid, m) + _dot(lo, m)


def _dot_exact_lhs(m, x):
  hi, mid, lo = _split3(x)
  return _dot(m, hi) + _dot(m, mid) + _dot(m, lo)


def _rmsnorm(x, g):
  ms = jnp.mean(x * x, axis=-1, keepdims=True)
  return x * lax.rsqrt(ms + NORM_EPS) * g


def _sigmoid(x):
  return 1.0 / (1.0 + jnp.exp(-x))


def _softplus(x):
  return jnp.maximum(x, 0.0) + jnp.log(1.0 + jnp.exp(-jnp.abs(x)))


def _causal_conv(p, prev, w, b):
  width = w.shape[0]
  cur = w[width - 1:width, :]
  y = p * cur + b
  head = jnp.concatenate([prev, p[0:SUBLANES, :]], axis=0)
  y_head = p[0:SUBLANES, :] * cur + b
  for k in range(1, width):
    tap = w[width - 1 - k:width - k, :]
    y = y + pltpu.roll(p, k, 0) * tap
    y_head = y_head + pltpu.roll(head, k, 0)[SUBLANES:2 * SUBLANES, :] * tap
  return jnp.concatenate([y_head, y[SUBLANES:, :]], axis=0)


def _ssd_in_kernel(h_ref, g_ref, wz_ref, wx_ref, wdt_ref, cw_ref, cb_ref,
                   dtb_ref, z_ref, xbc_ref, dt_ref, carry_ref, *, pf, n_heads):
  i = pl.program_id(1)
  tm = h_ref.shape[0]

  @pl.when(i == 0)
  def _():
    carry_ref[...] = jnp.zeros_like(carry_ref)

  u = _rmsnorm(h_ref[...], g_ref[...]).astype(BF16)
  z_ref[...] = _dot(u, wz_ref[...]).astype(BF16)

  dt = _softplus(_dot(u, wdt_ref[...]) + dtb_ref[...])
  row = i * tm + lax.broadcasted_iota(jnp.int32, dt.shape, 0)
  lane = lax.broadcasted_iota(jnp.int32, dt.shape, 1)
  dt_ref[...] = jnp.where((row >= pf) & (lane < n_heads), dt, 0.0)

  n_ch = xbc_ref.shape[1]
  for c in range(n_ch // XBC_COL_TILE):
    cs = slice(c * XBC_COL_TILE, (c + 1) * XBC_COL_TILE)
    p = _dot(u, wx_ref[:, cs])
    y = _causal_conv(p, carry_ref[:, cs], cw_ref[:, cs], cb_ref[:, cs])
    carry_ref[:, cs] = p[tm - SUBLANES:tm, :]
    xbc_ref[:, cs] = (y * _sigmoid(y)).astype(BF16)


def _ssd_in(h, g, wz, wx, wdt, cw, cb, dtb, *, batch, lp, pf, n_heads):
  d = h.shape[1]
  n_z, n_x, n_dt = wz.shape[1], wx.shape[1], wdt.shape[1]
  nt = lp // ROW_TILE
  rows = lambda n: pl.BlockSpec((ROW_TILE, n), lambda b, i: (b * nt + i, 0))
  return pl.pallas_call(
      functools.partial(_ssd_in_kernel, pf=pf, n_heads=n_heads),
      grid=(batch, nt),
      in_specs=[rows(d), _resident((1, d)), _resident(wz.shape),
                _resident(wx.shape), _resident(wdt.shape), _resident(cw.shape),
                _resident(cb.shape), _resident(dtb.shape)],
      out_specs=[rows(n_z), rows(n_x), rows(n_dt)],
      out_shape=[jax.ShapeDtypeStruct((batch * lp, n_z), BF16),
                 jax.ShapeDtypeStruct((batch * lp, n_x), BF16),
                 jax.ShapeDtypeStruct((batch * lp, n_dt), F32)],
      scratch_shapes=[pltpu.VMEM((SUBLANES, n_x), F32)],
      compiler_params=pltpu.CompilerParams(
          dimension_semantics=("arbitrary", "arbitrary"),
          vmem_limit_bytes=VMEM_LIMIT),
      name="ssd_in",
  )(h, g, wz, wx, wdt, cw, cb, dtb)


def _ssd_kernel(xbc_ref, bt_ref, z_ref, dt_ref, dtt_ref, h_ref, alog_row_ref,
                alog_col_ref, dskip_ref, gg_ref, wout_ref, expand_ref, o_ref,
                state_ref, *, d_inner):
  c = pl.program_id(1)
  q = h_ref.shape[0]
  n_state = SSM_STATE
  g_width = d_inner // SSM_GROUPS
  pairs_per_group = g_width // LANES

  @pl.when(c == 0)
  def _():
    state_ref[...] = jnp.zeros_like(state_ref)

  ri = lax.broadcasted_iota(jnp.int32, (q, q), 0)
  ci = lax.broadcasted_iota(jnp.int32, (q, q), 1)
  causal = ri >= ci
  lower = causal.astype(BF16)
  upper = (ri <= ci).astype(BF16)

  dt = dt_ref[...]
  acs = _dot_exact_lhs(lower, dt * -jnp.exp(alog_row_ref[...]))
  acs_t = _dot_exact_rhs(dtt_ref[...] * -jnp.exp(alog_col_ref[...]), upper)
  a_last = acs[q - 1:q, :]
  decay_to_end = jnp.exp(a_last - acs)
  decay_from_start = jnp.exp(acs)

  expand = expand_ref[...]
  dt_x = _dot(dt.astype(BF16), expand)
  dte_x = _dot(decay_to_end.astype(BF16), expand)
  dfs_x = _dot(decay_from_start.astype(BF16), expand)
  chunk_decay_x = _dot_exact_rhs(
      decay_from_start[q - SUBLANES:q, :], expand)[SUBLANES - 1:SUBLANES, :]

  xs = xbc_ref[:, 0:d_inner].astype(F32)
  xdt = xs * dt_x
  xdt_end = (xdt * dte_x).astype(BF16)
  xdt_b = xdt.astype(BF16)
  even_head = (lax.broadcasted_iota(jnp.int32, (1, d_inner), 1) % LANES
               ) < SSM_HEAD_DIM
  zero = jnp.zeros_like(xdt_b)
  xdt_even = jnp.where(even_head, xdt_b, zero)
  xdt_odd = jnp.where(even_head, zero, xdt_b)

  c_off = d_inner + SSM_GROUPS * n_state
  y_parts = []
  for g in range(SSM_GROUPS):
    gs = slice(g * g_width, (g + 1) * g_width)
    c_g = xbc_ref[:, c_off + g * n_state:c_off + (g + 1) * n_state]
    bt_g = bt_ref[g * n_state:(g + 1) * n_state, :]
    cb = _dot(c_g, bt_g)
    s_prev = state_ref[g]
    y_off = _dot(c_g, s_prev.astype(BF16)) * dfs_x[:, gs]
    state_ref[g] = s_prev * chunk_decay_x[:, gs] + _dot(bt_g, xdt_end[:, gs])
    pair_out = []
    for pr in range(pairs_per_group):
      cs = slice(g * g_width + pr * LANES, g * g_width + (pr + 1) * LANES)
      acc = None
      for half, xsrc in ((0, xdt_even), (1, xdt_odd)):
        hd = (g * g_width + pr * LANES) // SSM_HEAD_DIM + half
        diff = acs[:, hd:hd + 1] - acs_t[hd:hd + 1, :]
        decay = jnp.exp(jnp.where(causal, diff, -jnp.inf))
        part = _dot((cb * decay).astype(BF16), xsrc[:, cs])
        acc = part if acc is None else acc + part
      pair_out.append(acc)
    y_parts.append(jnp.concatenate(pair_out, axis=1) + y_off)

  zf = z_ref[...].astype(F32)
  gate = zf * _sigmoid(zf)
  normed = []
  for g in range(SSM_GROUPS):
    gs = slice(g * g_width, (g + 1) * g_width)
    hg = (y_parts[g] + xs[:, gs] * dskip_ref[:, gs]) * gate[:, gs]
    normed.append(_rmsnorm(hg, gg_ref[:, gs]).astype(BF16))
  hg_all = jnp.concatenate(normed, axis=1)
  o_ref[...] = h_ref[...] + _dot(hg_all, wout_ref[...])


def _ssd(xbc, bt, z, dt, dtt, h, alog_row, alog_col, dskip_x, gate_g, wout,
         expand, *, batch, lp):
  d = h.shape[1]
  d_inner = z.shape[1]
  q = SSM_CHUNK
  nc = lp // q
  n_heads = dtt.shape[1]
  rows = lambda n: pl.BlockSpec((q, n), lambda b, c: (b * nc + c, 0))
  cols = lambda n: pl.BlockSpec((None, n, q), lambda b, c: (b, 0, c))
  return pl.pallas_call(
      functools.partial(_ssd_kernel, d_inner=d_inner),
      grid=(batch, nc),
      in_specs=[rows(xbc.shape[1]), cols(bt.shape[1]), rows(d_inner),
                rows(dt.shape[1]), cols(n_heads), rows(d),
                _resident(alog_row.shape), _resident(alog_col.shape),
                _resident(dskip_x.shape), _resident(gate_g.shape),
                _resident(wout.shape), _resident(expand.shape)],
      out_specs=rows(d),
      out_shape=jax.ShapeDtypeStruct((batch * lp, d), F32),
      scratch_shapes=[pltpu.VMEM(
          (SSM_GROUPS, SSM_STATE, d_inner // SSM_GROUPS), F32)],
      compiler_params=pltpu.CompilerParams(
          dimension_semantics=("arbitrary", "arbitrary"),
          vmem_limit_bytes=VMEM_LIMIT),
      name="ssd_scan",
  )(xbc, bt, z, dt, dtt, h, alog_row, alog_col, dskip_x, gate_g, wout, expand)


def _ffn_kernel(*refs, pf, d_ff, has_attn, has_final):
  refs = list(refs)
  h_ref = refs.pop(0)
  if has_attn:
    attn_ref, wo_ref = refs.pop(0), refs.pop(0)
  g_ref, wup_ref, cw_ref, cb_ref, wdn_ref = (refs.pop(0) for _ in range(5))
  if has_final:
    fin_ref = refs.pop(0)
  o_ref, carry_ref = refs
  i = pl.program_id(1)
  tm = h_ref.shape[0]

  @pl.when(i == 0)
  def _():
    carry_ref[...] = jnp.zeros_like(carry_ref)

  h = h_ref[...]
  if has_attn:
    h = h + _dot(attn_ref[...], wo_ref[...])
  u = _rmsnorm(h, g_ref[...]).astype(BF16)
  acc = jnp.zeros(h.shape, F32)
  for c in range(d_ff // FFN_COL_TILE):
    gs = slice(c * FFN_COL_TILE, (c + 1) * FFN_COL_TILE)
    vs = slice(d_ff + c * FFN_COL_TILE, d_ff + (c + 1) * FFN_COL_TILE)
    pg = _dot(u, wup_ref[:, gs])
    pv = _dot(u, wup_ref[:, vs])
    yg = _causal_conv(pg, carry_ref[:, gs], cw_ref[:, gs], cb_ref[:, gs])
    yv = _causal_conv(pv, carry_ref[:, vs], cw_ref[:, vs], cb_ref[:, vs])
    carry_ref[:, gs] = pg[tm - SUBLANES:tm, :]
    carry_ref[:, vs] = pv[tm - SUBLANES:tm, :]
    act = (yg * _sigmoid(yg) * yv).astype(BF16)
    acc = acc + _dot(act, wdn_ref[gs, :])
  out = h + acc
  row = i * tm + lax.broadcasted_iota(jnp.int32, out.shape, 0)
  out = jnp.where(row >= pf, out, 0.0)
  if has_final:
    out = _rmsnorm(out, fin_ref[...])
  o_ref[...] = out


def _ffn(h, attn, wo, g, wup, cw, cb, wdn, fin_g, *, batch, lp, pf):
  d = h.shape[1]
  d_ff = wdn.shape[0]
  nt = lp // ROW_TILE
  rows = lambda n: pl.BlockSpec((ROW_TILE, n), lambda b, i: (b * nt + i, 0))
  has_attn = attn is not None
  has_final = fin_g is not None
  args, specs = [h], [rows(d)]
  if has_attn:
    args += [attn, wo]
    specs += [rows(attn.shape[1]), _resident(wo.shape)]
  args += [g, wup, cw, cb, wdn]
  specs += [_resident(a.shape) for a in (g, wup, cw, cb, wdn)]
  if has_final:
    args.append(fin_g)
    specs.append(_resident(fin_g.shape))
  return pl.pallas_call(
      functools.partial(_ffn_kernel, pf=pf, d_ff=d_ff, has_attn=has_attn,
                        has_final=has_final),
      grid=(batch, nt),
      in_specs=specs,
      out_specs=rows(d),
      out_shape=jax.ShapeDtypeStruct((batch * lp, d), F32),
      scratch_shapes=[pltpu.VMEM((SUBLANES, 2 * d_ff), F32)],
      compiler_params=pltpu.CompilerParams(
          dimension_semantics=("arbitrary", "arbitrary"),
          vmem_limit_bytes=VMEM_LIMIT),
      name="conv_ffn",
  )(*args)


def _qkv_kernel(h_ref, gkv_ref, gq_ref, wkv_ref, wq_ref, kv_ref, q_ref, *,
                q_scale):
  h = h_ref[...]
  hn = h * lax.rsqrt(jnp.mean(h * h, axis=-1, keepdims=True) + NORM_EPS)
  kv_ref[...] = _dot((hn * gkv_ref[...]).astype(BF16), wkv_ref[...]).astype(BF16)
  q = _dot((hn * gq_ref[...]).astype(BF16), wq_ref[...])
  q_ref[...] = (q * q_scale).astype(BF16)


def _qkv(h, gkv, gq, wkv, wq, *, q_scale):
  n, d = h.shape
  rows = lambda m: pl.BlockSpec((ROW_TILE, m), lambda i: (i, 0))
  return pl.pallas_call(
      functools.partial(_qkv_kernel, q_scale=q_scale),
      grid=(n // ROW_TILE,),
      in_specs=[rows(d), _resident(gkv.shape), _resident(gq.shape),
                _resident(wkv.shape), _resident(wq.shape)],
      out_specs=[rows(wkv.shape[1]), rows(wq.shape[1])],
      out_shape=[jax.ShapeDtypeStruct((n, wkv.shape[1]), BF16),
                 jax.ShapeDtypeStruct((n, wq.shape[1]), BF16)],
      compiler_params=pltpu.CompilerParams(
          dimension_semantics=("arbitrary",), vmem_limit_bytes=VMEM_LIMIT),
      name="qkv_proj",
  )(h, gkv, gq, wkv, wq)


def _attn_kernel(q_ref, k_ref, v_ref, o_ref, acc_ref, r_ref, *, pf):
  i = pl.program_id(2)
  t = q_ref.shape[0]
  n_pairs = q_ref.shape[1] // LANES
  first_head = lax.broadcasted_iota(jnp.int32, (1, LANES), 1) < SB_HEAD_DIM
  q_heads = []
  for p in range(n_pairs):
    q = q_ref[:, p * LANES:(p + 1) * LANES]
    zq = jnp.zeros_like(q)
    q_heads += [jnp.where(first_head, q, zq), jnp.where(first_head, zq, q)]
  ri = lax.broadcasted_iota(jnp.int32, (t, t), 0)
  ci = lax.broadcasted_iota(jnp.int32, (t, t), 1)
  neg_suffix = jnp.where(ri >= ci, -1.0, 0.0).astype(BF16)

  def block(j, masked, carry):
    accs, r_heads = list(carry[:n_pairs]), carry[n_pairs:]
    start = pl.multiple_of(j * t, t)
    if masked:
      s_idx = j * t + ci
      visible = (s_idx < i * t + ri) & (s_idx >= pf)
    new_r = []
    for hd, (qa, ra) in enumerate(zip(q_heads, r_heads)):
      p, half = divmod(hd, 2)
      kb = k_ref[pl.ds(start, t), p * LANES:(p + 1) * LANES]
      vb = v_ref[pl.ds(start, t), p * LANES:(p + 1) * LANES]
      zv = jnp.zeros_like(vb)
      va = (jnp.where(first_head, vb, zv) if half == 0
            else jnp.where(first_head, zv, vb))
      y = lax.dot_general(qa, kb, (((1,), (1,)), ((), ())),
                          preferred_element_type=F32)
      sp = jnp.maximum(y, jnp.log2(1.0 + jnp.exp2(jnp.minimum(y, 126.0))))
      if masked:
        sp = jnp.where(visible, sp, 0.0)
      incl = _dot(sp.astype(BF16), neg_suffix)
      arg = y + ra + incl
      if masked:
        arg = jnp.where(visible, arg, -jnp.inf)
      accs[p] = accs[p] + _dot(jnp.exp2(arg.astype(BF16)), va)
      new_r.append(ra + incl[:, 0:1])
    return (*accs, *new_r)

  def max_carried(cr):
    m = cr[n_pairs]
    for r in cr[n_pairs + 1:]:
      m = jnp.maximum(m, r)
    return jnp.max(m)

  def live(state):
    return (state[0] < jnp.maximum(i - 1, 0)) & (state[1] > ATTN_DEAD_LOG2)

  def step(state):
    out = block(i - 1 - state[0], False, state[2:])
    return (state[0] + 1, max_carried(out), *out)

  zero_r = jnp.zeros((t, 1), F32)
  carry = block(i, True, (jnp.zeros((t, LANES), F32),) * n_pairs
                + (zero_r,) * (2 * n_pairs))
  state = lax.while_loop(live, step, (jnp.int32(0), max_carried(carry), *carry))

  def stash(cr):
    for p in range(n_pairs):
      acc_ref[:, p * LANES:(p + 1) * LANES] = cr[p]
    for hd in range(2 * n_pairs):
      r_ref[hd] = cr[n_pairs + hd]

  stash(state[2:])

  @pl.when((i >= 1) & (state[1] > ATTN_DEAD_LOG2))
  def _():
    stash(block(0, True,
                tuple(acc_ref[:, p * LANES:(p + 1) * LANES] for p in range(n_pairs))
                + tuple(r_ref[hd] for hd in range(2 * n_pairs))))

  o_ref[...] = acc_ref[...].astype(BF16)


def _attention(q, kv, *, batch, lp, pf):
  n, width = q.shape
  t = ATTN_TILE
  nq = lp // t
  bw = ATTN_PAIRS * LANES
  n_col = width // bw
  return pl.pallas_call(
      functools.partial(_attn_kernel, pf=pf),
      grid=(batch, n_col, nq),
      in_specs=[
          pl.BlockSpec((t, bw), lambda b, p, i: (b * nq + i, p)),
          pl.BlockSpec((lp, bw), lambda b, p, i: (b, p)),
          pl.BlockSpec((lp, bw), lambda b, p, i: (b, n_col + p)),
      ],
      out_specs=pl.BlockSpec((t, bw), lambda b, p, i: (b * nq + i, p)),
      out_shape=jax.ShapeDtypeStruct((n, width), BF16),
      scratch_shapes=[pltpu.VMEM((t, bw), F32),
                      pltpu.VMEM((2 * ATTN_PAIRS, t, 1), F32)],
      compiler_params=pltpu.CompilerParams(
          dimension_semantics=("arbitrary", "arbitrary", "arbitrary"),
          vmem_limit_bytes=VMEM_LIMIT),
      name="sb_attention",
  )(q, kv, kv)


def _row(v, width=None):
  v = v.astype(F32).reshape(1, -1)
  if width is not None and v.shape[1] < width:
    v = jnp.pad(v, ((0, 0), (0, width - v.shape[1])))
  return v


def kernel(x, meta_tokens, ssd_norm, ssd_w_in, ssd_conv_w, ssd_conv_b, ssd_dt_bias, ssd_a_log, ssd_d_skip, ssd_gate_norm, ssd_w_out, kv_norm, w_kv, sb_norm, sb_w_q, sb_w_o, ffn_norm, ffn_w_up, ffn_conv_w, ffn_conv_b, ffn_w_down, final_norm):
  batch, seq, d = x.shape
  depth = ffn_norm.shape[0]
  n_a = ssd_norm.shape[0]
  d_inner = ssd_w_out.shape[1]
  n_heads = ssd_a_log.shape[1]
  conv_dim = ssd_conv_w.shape[2]
  n_meta = meta_tokens.shape[0]
  length = n_meta + seq
  pf = (-n_meta) % SSM_CHUNK
  lp = pf + length
  assert lp % SSM_CHUNK == 0 and lp % ROW_TILE == 0 and lp % ATTN_TILE == 0
  assert n_heads <= LANES and d_inner == n_heads * SSM_HEAD_DIM
  dims = dict(batch=batch, lp=lp)

  h = jnp.concatenate([
      jnp.zeros((batch, pf, d), F32),
      jnp.broadcast_to(meta_tokens[None].astype(F32), (batch, n_meta, d)),
      x.astype(F32)], axis=1).reshape(batch * lp, d)

  expand = (jnp.arange(d_inner)[None, :] // SSM_HEAD_DIM
            == jnp.arange(LANES)[:, None]).astype(BF16)

  kv = None
  for layer in range(depth):
    attn = wo = None
    if layer < n_a:
      w_in = ssd_w_in[layer].astype(BF16)
      wz = w_in[:, :d_inner]
      wx = w_in[:, d_inner:d_inner + conv_dim]
      wdt = jnp.pad(w_in[:, d_inner + conv_dim:], ((0, 0), (0, LANES - n_heads)))
      z, xbc, dt = _ssd_in(
          h, _row(ssd_norm[layer]), wz, wx, wdt, ssd_conv_w[layer].astype(F32),
          _row(ssd_conv_b[layer]), _row(ssd_dt_bias[layer], LANES),
          pf=pf, n_heads=n_heads, **dims)
      b_off = d_inner
      bt = jnp.swapaxes(
          xbc[:, b_off:b_off + SSM_GROUPS * SSM_STATE].reshape(batch, lp, -1), 1, 2)
      dtt = jnp.swapaxes(dt[:, :n_heads].reshape(batch, lp, n_heads), 1, 2)
      h = _ssd(xbc, bt, z, dt, dtt, h, _row(ssd_a_log[layer], LANES),
               ssd_a_log[layer].astype(F32).reshape(n_heads, 1),
               _row(jnp.repeat(ssd_d_skip[layer], SSM_HEAD_DIM)),
               _row(ssd_gate_norm[layer]), ssd_w_out[layer].astype(BF16),
               expand, **dims)
    else:
      j = layer - n_a
      if layer == n_a:
        kv, q = _qkv(h, _row(kv_norm), _row(sb_norm[j]), w_kv.astype(BF16),
                     sb_w_q[j].astype(BF16), q_scale=SB_HEAD_DIM ** -0.5 * LOG2_E)
      else:
        _, q = _qkv(h, _row(kv_norm), _row(sb_norm[j]), w_kv.astype(BF16),
                    sb_w_q[j].astype(BF16), q_scale=SB_HEAD_DIM ** -0.5 * LOG2_E)
      attn = _attention(q, kv, pf=pf, **dims)
      wo = sb_w_o[j].astype(BF16)
    fin_g = _row(final_norm) if layer == depth - 1 else None
    h = _ffn(h, attn, wo, _row(ffn_norm[layer]), ffn_w_up[layer].astype(BF16),
             ffn_conv_w[layer].astype(F32), _row(ffn_conv_b[layer]),
             ffn_w_down[layer].astype(BF16), fin_g, pf=pf, **dims)
  return h.reshape(batch, lp, d)[:, pf + n_meta:].astype(x.dtype)
```

```python
import functools

import jax
import jax.numpy as jnp
from jax import lax
from jax.experimental import pallas as pl
from jax.experimental.pallas import tpu as pltpu

F32 = jnp.float32
BF16 = jnp.bfloat16

N_META = 16
NORM_EPS = 1e-6
SSM_HEAD_DIM = 64
SSM_GROUPS = 4
SSM_STATE = 128
SSM_CHUNK = 256
SB_HEAD_DIM = 64
LOG2_E = 1.4426950408889634

LANES = 128
SUBLANES = 8
IN_ROW_TILE = 384
FFN_ROW_TILE = 384
QKV_ROW_TILE = 768
FFN_COL_TILE = 256
XBC_COL_TILE = 512
STAGE_SLOTS = 2
ATTN_TILE = 256
ATTN_PAIRS = 2
ATTN_DEAD_LOG2 = -160.0
VMEM_LIMIT = 56 * 1024 * 1024


def _resident(shape):
  nd = len(shape)
  return pl.BlockSpec(shape, lambda *_: (0,) * nd, pipeline_mode=pl.Buffered(1))


def _dot(a, b):
  return jnp.dot(a, b, preferred_element_type=F32)


def _split3(x):
  hi = x.astype(BF16)
  r1 = x - hi.astype(F32)
  mid = r1.astype(BF16)
  lo = (r1 - mid.astype(F32)).astype(BF16)
  return hi, mid, lo


def _dot_exact_rhs(x, m):
  hi, mid, lo = _split3(x)
  return _dot(hi, m) + _dot(mid, m) + _dot(lo, m)


def _dot_exact_lhs(m, x):
  hi, mid, lo = _split3(x)
  return _dot(m, hi) + _dot(m, mid) + _dot(m, lo)


def _rmsnorm(x, g):
  ms = jnp.mean(x * x, axis=-1, keepdims=True)
  return x * lax.rsqrt(ms + NORM_EPS) * g


def _sigmoid(x):
  return 1.0 / (1.0 + jnp.exp(-x))


def _softplus(x):
  return jnp.maximum(x, 0.0) + jnp.log(1.0 + jnp.exp(-jnp.abs(x)))


def _causal_conv(p, prev, w, b, stage_ref):
  tm = p.shape[0]
  width = w.shape[0]
  stage_ref[0:SUBLANES, :] = prev
  stage_ref[SUBLANES:SUBLANES + tm, :] = p
  y = p * w[width - 1:width, :] + b
  for k in range(1, width):
    y = y + (stage_ref[SUBLANES - k:SUBLANES - k + tm, :]
             * w[width - 1 - k:width - k, :])
  return y


def _ssd_in_kernel(h_ref, g_ref, w_ref, cw_ref, cb_ref, dtb_ref, z_ref, xbc_ref,
                   dt_ref, bt_ref, dtt_ref, carry_ref, stage_ref, *, pf,
                   n_heads, b_off):
  i = pl.program_id(1)
  tm = h_ref.shape[0]
  x_off = z_ref.shape[1]
  dt_off = x_off + xbc_ref.shape[1]
  assert b_off % XBC_COL_TILE == 0 and bt_ref.shape[0] == XBC_COL_TILE
  b_step = b_off // XBC_COL_TILE

  @pl.when(i == 0)
  def _():
    carry_ref[...] = jnp.zeros_like(carry_ref)

  u = _rmsnorm(h_ref[...], g_ref[...]).astype(BF16)

  dt = _softplus(_dot(u, w_ref[:, dt_off:dt_off + LANES]) + dtb_ref[...])
  row = i * tm + lax.broadcasted_iota(jnp.int32, dt.shape, 0)
  lane = lax.broadcasted_iota(jnp.int32, dt.shape, 1)
  dt = jnp.where((row >= pf) & (lane < n_heads), dt, 0.0)
  dt_ref[...] = dt
  dtt_ref[...] = dt.T[0:dtt_ref.shape[0], :]

  n_steps = xbc_ref.shape[1] // XBC_COL_TILE
  cols = lambda c: slice(c * XBC_COL_TILE, (c + 1) * XBC_COL_TILE)
  n_z = z_ref.shape[1]
  assert n_z % XBC_COL_TILE == 0 and n_z // XBC_COL_TILE <= n_steps
  xcols = lambda c: slice(x_off + c * XBC_COL_TILE,
                          x_off + (c + 1) * XBC_COL_TILE)
  nxt = _dot(u, w_ref[:, xcols(0)])
  for c in range(n_steps):
    cs = cols(c)
    p = nxt
    if c + 1 < n_steps:
      nxt = _dot(u, w_ref[:, xcols(c + 1)])
    if c < n_z // XBC_COL_TILE:
      z_ref[:, cs] = _dot(u, w_ref[:, cs]).astype(BF16)
    y = _causal_conv(p, carry_ref[:, cs], cw_ref[:, cs], cb_ref[:, cs],
                     stage_ref.at[c % STAGE_SLOTS])
    carry_ref[:, cs] = p[tm - SUBLANES:tm, :]
    act = y * _sigmoid(y)
    xbc_ref[:, cs] = act.astype(BF16)
    if c == b_step:
      bt_ref[...] = act.T.astype(BF16)


def _ssd_in(h, g, w, cw, cb, dtb, *, batch, lp, pf, n_heads, n_z):
  d = h.shape[1]
  n_x, n_dt = cw.shape[1], LANES
  assert w.shape[1] == n_z + n_x + n_dt
  tm = IN_ROW_TILE
  nt = lp // tm
  rows = lambda n: pl.BlockSpec((tm, n), lambda b, i: (b * nt + i, 0))
  cols = lambda n: pl.BlockSpec((None, n, tm), lambda b, i: (b, 0, i))
  n_b = SSM_GROUPS * SSM_STATE
  return pl.pallas_call(
      functools.partial(_ssd_in_kernel, pf=pf, n_heads=n_heads,
                        b_off=n_x - 2 * n_b),
      grid=(batch, nt),
      in_specs=[rows(d), _resident((1, d)), _resident(w.shape),
                _resident(cw.shape), _resident(cb.shape), _resident(dtb.shape)],
      out_specs=[rows(n_z), rows(n_x), rows(n_dt), cols(n_b), cols(n_heads)],
      out_shape=[jax.ShapeDtypeStruct((batch * lp, n_z), BF16),
                 jax.ShapeDtypeStruct((batch * lp, n_x), BF16),
                 jax.ShapeDtypeStruct((batch * lp, n_dt), F32),
                 jax.ShapeDtypeStruct((batch, n_b, lp), BF16),
                 jax.ShapeDtypeStruct((batch, n_heads, lp), F32)],
      scratch_shapes=[
          pltpu.VMEM((SUBLANES, n_x), F32),
          pltpu.VMEM((STAGE_SLOTS, tm + SUBLANES, XBC_COL_TILE), F32)],
      compiler_params=pltpu.CompilerParams(
          dimension_semantics=("arbitrary", "arbitrary"),
          vmem_limit_bytes=VMEM_LIMIT),
      name="ssd_in",
  )(h, g, w, cw, cb, dtb)


def _ssd_kernel(xbc_ref, bt_ref, z_ref, dt_ref, dtt_ref, h_ref, alog_row_ref,
                alog_col_ref, dskip_ref, gg_ref, wout_ref, expand_ref, o_ref,
                state_ref, *, d_inner):
  c = pl.program_id(1)
  q = h_ref.shape[0]
  n_state = SSM_STATE
  g_width = d_inner // SSM_GROUPS
  pairs_per_group = g_width // LANES

  @pl.when(c == 0)
  def _():
    state_ref[...] = jnp.zeros_like(state_ref)

  ri = lax.broadcasted_iota(jnp.int32, (q, q), 0)
  ci = lax.broadcasted_iota(jnp.int32, (q, q), 1)
  causal = ri >= ci
  lower = causal.astype(BF16)
  upper = (ri <= ci).astype(BF16)

  dt = dt_ref[...]
  acs = _dot_exact_lhs(lower, dt * -jnp.exp(alog_row_ref[...]))
  acs_t = _dot_exact_rhs(dtt_ref[...] * -jnp.exp(alog_col_ref[...]), upper)
  a_last = acs[q - 1:q, :]
  decay_to_end = jnp.exp(a_last - acs)
  decay_from_start = jnp.exp(acs)

  expand = expand_ref[...]
  dt_x = _dot(dt.astype(BF16), expand)
  dte_x = _dot(decay_to_end.astype(BF16), expand)
  dfs_x = _dot(decay_from_start.astype(BF16), expand)
  chunk_decay_x = _dot_exact_rhs(
      decay_from_start[q - SUBLANES:q, :], expand)[SUBLANES - 1:SUBLANES, :]

  xs = xbc_ref[:, 0:d_inner].astype(F32)
  xdt = xs * dt_x
  xdt_end = (xdt * dte_x).astype(BF16)
  xdt_b = xdt.astype(BF16)
  even_head = (lax.broadcasted_iota(jnp.int32, (1, d_inner), 1) % LANES
               ) < SSM_HEAD_DIM
  zero = jnp.zeros_like(xdt_b)
  xdt_even = jnp.where(even_head, xdt_b, zero)
  xdt_odd = jnp.where(even_head, zero, xdt_b)

  c_off = d_inner + SSM_GROUPS * n_state
  y_parts = []
  for g in range(SSM_GROUPS):
    gs = slice(g * g_width, (g + 1) * g_width)
    c_g = xbc_ref[:, c_off + g * n_state:c_off + (g + 1) * n_state]
    bt_g = bt_ref[g * n_state:(g + 1) * n_state, :]
    cb = _dot(c_g, bt_g)
    s_prev = state_ref[g]
    y_off = _dot(c_g, s_prev.astype(BF16)) * dfs_x[:, gs]
    state_ref[g] = s_prev * chunk_decay_x[:, gs] + _dot(bt_g, xdt_end[:, gs])
    pair_out = []
    for pr in range(pairs_per_group):
      cs = slice(g * g_width + pr * LANES, g * g_width + (pr + 1) * LANES)
      acc = None
      for half, xsrc in ((0, xdt_even), (1, xdt_odd)):
        hd = (g * g_width + pr * LANES) // SSM_HEAD_DIM + half
        diff = acs[:, hd:hd + 1] - acs_t[hd:hd + 1, :]
        decay = jnp.exp(jnp.where(causal, diff, -jnp.inf))
        part = _dot((cb * decay).astype(BF16), xsrc[:, cs])
        acc = part if acc is None else acc + part
      pair_out.append(acc)
    y_parts.append(jnp.concatenate(pair_out, axis=1) + y_off)

  zf = z_ref[...].astype(F32)
  gate = zf * _sigmoid(zf)
  normed = []
  for g in range(SSM_GROUPS):
    gs = slice(g * g_width, (g + 1) * g_width)
    hg = (y_parts[g] + xs[:, gs] * dskip_ref[:, gs]) * gate[:, gs]
    normed.append(_rmsnorm(hg, gg_ref[:, gs]).astype(BF16))
  hg_all = jnp.concatenate(normed, axis=1)
  o_ref[...] = h_ref[...] + _dot(hg_all, wout_ref[...])


def _ssd(xbc, bt, z, dt, dtt, h, alog_row, alog_col, dskip_x, gate_g, wout,
         expand, *, batch, lp):
  d = h.shape[1]
  d_inner = z.shape[1]
  q = SSM_CHUNK
  nc = lp // q
  n_heads = dtt.shape[1]
  rows = lambda n: pl.BlockSpec((q, n), lambda b, c: (b * nc + c, 0))
  cols = lambda n: pl.BlockSpec((None, n, q), lambda b, c: (b, 0, c))
  return pl.pallas_call(
      functools.partial(_ssd_kernel, d_inner=d_inner),
      grid=(batch, nc),
      in_specs=[rows(xbc.shape[1]), cols(bt.shape[1]), rows(d_inner),
                rows(dt.shape[1]), cols(n_heads), rows(d),
                _resident(alog_row.shape), _resident(alog_col.shape),
                _resident(dskip_x.shape), _resident(gate_g.shape),
                _resident(wout.shape), _resident(expand.shape)],
      out_specs=rows(d),
      out_shape=jax.ShapeDtypeStruct((batch * lp, d), F32),
      scratch_shapes=[pltpu.VMEM(
          (SSM_GROUPS, SSM_STATE, d_inner // SSM_GROUPS), F32)],
      compiler_params=pltpu.CompilerParams(
          dimension_semantics=("arbitrary", "arbitrary"),
          vmem_limit_bytes=VMEM_LIMIT),
      name="ssd_scan",
  )(xbc, bt, z, dt, dtt, h, alog_row, alog_col, dskip_x, gate_g, wout, expand)


def _ffn_kernel(*refs, pf, d_ff, has_attn, has_final):
  refs = list(refs)
  h_ref = refs.pop(0)
  if has_attn:
    attn_ref, wo_ref = refs.pop(0), refs.pop(0)
  g_ref, wup_ref, cw_ref, cb_ref, wdn_ref = (refs.pop(0) for _ in range(5))
  if has_final:
    fin_ref = refs.pop(0)
  o_ref, carry_ref, stage_ref = refs
  i = pl.program_id(1)
  tm = h_ref.shape[0]

  @pl.when(i == 0)
  def _():
    carry_ref[...] = jnp.zeros_like(carry_ref)

  h = h_ref[...]
  if has_attn:
    h = h + _dot(attn_ref[...], wo_ref[...])
  u = _rmsnorm(h, g_ref[...]).astype(BF16)
  def cols(c):
    return (slice(c * FFN_COL_TILE, (c + 1) * FFN_COL_TILE),
            slice(d_ff + c * FFN_COL_TILE, d_ff + (c + 1) * FFN_COL_TILE))

  def up(c):
    gs, vs = cols(c)
    return _dot(u, wup_ref[:, gs]), _dot(u, wup_ref[:, vs])

  n_steps = d_ff // FFN_COL_TILE
  acc = jnp.zeros(h.shape, F32)
  nxt = up(0)
  act = None
  for c in range(n_steps):
    gs, vs = cols(c)
    pg, pv = nxt
    if c + 1 < n_steps:
      nxt = up(c + 1)
    if act is not None:
      acc = acc + _dot(act, wdn_ref[cols(c - 1)[0], :])
    slot = 2 * (c % STAGE_SLOTS)
    yg = _causal_conv(pg, carry_ref[:, gs], cw_ref[:, gs], cb_ref[:, gs],
                      stage_ref.at[slot])
    yv = _causal_conv(pv, carry_ref[:, vs], cw_ref[:, vs], cb_ref[:, vs],
                      stage_ref.at[slot + 1])
    carry_ref[:, gs] = pg[tm - SUBLANES:tm, :]
    carry_ref[:, vs] = pv[tm - SUBLANES:tm, :]
    act = (yg * _sigmoid(yg) * yv).astype(BF16)
  acc = acc + _dot(act, wdn_ref[cols(n_steps - 1)[0], :])
  out = h + acc
  row = i * tm + lax.broadcasted_iota(jnp.int32, out.shape, 0)
  out = jnp.where(row >= pf, out, 0.0)
  if has_final:
    out = _rmsnorm(out, fin_ref[...])
  o_ref[...] = out


def _ffn(h, attn, wo, g, wup, cw, cb, wdn, fin_g, *, batch, lp, pf):
  d = h.shape[1]
  d_ff = wdn.shape[0]
  tm = FFN_ROW_TILE
  nt = lp // tm
  rows = lambda n: pl.BlockSpec((tm, n), lambda b, i: (b * nt + i, 0))
  has_attn = attn is not None
  has_final = fin_g is not None
  args, specs = [h], [rows(d)]
  if has_attn:
    args += [attn, wo]
    specs += [rows(attn.shape[1]), _resident(wo.shape)]
  args += [g, wup, cw, cb, wdn]
  specs += [_resident(a.shape) for a in (g, wup, cw, cb, wdn)]
  if has_final:
    args.append(fin_g)
    specs.append(_resident(fin_g.shape))
  return pl.pallas_call(
      functools.partial(_ffn_kernel, pf=pf, d_ff=d_ff, has_attn=has_attn,
                        has_final=has_final),
      grid=(batch, nt),
      in_specs=specs,
      out_specs=rows(d),
      out_shape=jax.ShapeDtypeStruct((batch * lp, d), F32),
      scratch_shapes=[
          pltpu.VMEM((SUBLANES, 2 * d_ff), F32),
          pltpu.VMEM((2 * STAGE_SLOTS, tm + SUBLANES, FFN_COL_TILE), F32)],
      compiler_params=pltpu.CompilerParams(
          dimension_semantics=("arbitrary", "arbitrary"),
          vmem_limit_bytes=VMEM_LIMIT),
      name="conv_ffn",
  )(*args)


def _qkv_kernel(h_ref, gkv_ref, gq_ref, wkv_ref, wq_ref, kv_ref, q_ref, *,
                q_scale):
  h = h_ref[...]
  hn = h * lax.rsqrt(jnp.mean(h * h, axis=-1, keepdims=True) + NORM_EPS)
  kv_ref[...] = _dot((hn * gkv_ref[...]).astype(BF16), wkv_ref[...]).astype(BF16)
  q = _dot((hn * gq_ref[...]).astype(BF16), wq_ref[...])
  q_ref[...] = (q * q_scale).astype(BF16)


def _qkv(h, gkv, gq, wkv, wq, *, q_scale):
  n, d = h.shape
  rows = lambda m: pl.BlockSpec((QKV_ROW_TILE, m), lambda i: (i, 0))
  return pl.pallas_call(
      functools.partial(_qkv_kernel, q_scale=q_scale),
      grid=(n // QKV_ROW_TILE,),
      in_specs=[rows(d), _resident(gkv.shape), _resident(gq.shape),
                _resident(wkv.shape), _resident(wq.shape)],
      out_specs=[rows(wkv.shape[1]), rows(wq.shape[1])],
      out_shape=[jax.ShapeDtypeStruct((n, wkv.shape[1]), BF16),
                 jax.ShapeDtypeStruct((n, wq.shape[1]), BF16)],
      compiler_params=pltpu.CompilerParams(
          dimension_semantics=("arbitrary",), vmem_limit_bytes=VMEM_LIMIT),
      name="qkv_proj",
  )(h, gkv, gq, wkv, wq)


def _attn_kernel(q_ref, k_ref, v_ref, o_ref, acc_ref, r_ref, *, pf):
  i = pl.program_id(2)
  t = q_ref.shape[0]
  n_pairs = q_ref.shape[1] // LANES
  first_head = lax.broadcasted_iota(jnp.int32, (1, LANES), 1) < SB_HEAD_DIM
  q_heads = []
  for p in range(n_pairs):
    q = q_ref[:, p * LANES:(p + 1) * LANES]
    zq = jnp.zeros_like(q)
    q_heads += [jnp.where(first_head, q, zq), jnp.where(first_head, zq, q)]
  ri = lax.broadcasted_iota(jnp.int32, (t, t), 0)
  ci = lax.broadcasted_iota(jnp.int32, (t, t), 1)
  neg_suffix = jnp.where(ri >= ci, -1.0, 0.0).astype(BF16)

  def block(j, masked, carry):
    accs, r_heads = list(carry[:n_pairs]), carry[n_pairs:]
    start = pl.multiple_of(j * t, t)
    if masked:
      s_idx = j * t + ci
      visible = (s_idx < i * t + ri) & (s_idx >= pf)
    logits = []
    for hd, qa in enumerate(q_heads):
      p = hd // 2
      kb = k_ref[pl.ds(start, t), p * LANES:(p + 1) * LANES]
      logits.append(lax.dot_general(qa, kb, (((1,), (1,)), ((), ())),
                                    preferred_element_type=F32))
    suffix_sums = []
    for y in logits:
      sp = jnp.maximum(y, jnp.log2(1.0 + jnp.exp2(jnp.minimum(y, 126.0))))
      if masked:
        sp = jnp.where(visible, sp, 0.0)
      suffix_sums.append(_dot(sp.astype(BF16), neg_suffix))
    new_r = []
    for hd, (y, incl, ra) in enumerate(zip(logits, suffix_sums, r_heads)):
      p, half = divmod(hd, 2)
      vb = v_ref[pl.ds(start, t), p * LANES:(p + 1) * LANES]
      zv = jnp.zeros_like(vb)
      va = (jnp.where(first_head, vb, zv) if half == 0
            else jnp.where(first_head, zv, vb))
      arg = y + ra + incl
      if masked:
        arg = jnp.where(visible, arg, -jnp.inf)
      accs[p] = accs[p] + _dot(jnp.exp2(arg.astype(BF16)), va)
      new_r.append(ra + incl[:, 0:1])
    return (*accs, *new_r)

  def max_carried(cr):
    m = cr[n_pairs]
    for r in cr[n_pairs + 1:]:
      m = jnp.maximum(m, r)
    return jnp.max(m)

  def live(state):
    return (state[0] < jnp.maximum(i - 1, 0)) & (state[1] > ATTN_DEAD_LOG2)

  def step(state):
    out = block(i - 1 - state[0], False, state[2:])
    return (state[0] + 1, max_carried(out), *out)

  zero_r = jnp.zeros((t, 1), F32)
  carry = block(i, True, (jnp.zeros((t, LANES), F32),) * n_pairs
                + (zero_r,) * (2 * n_pairs))
  state = lax.while_loop(live, step, (jnp.int32(0), max_carried(carry), *carry))

  def stash(cr):
    for p in range(n_pairs):
      acc_ref[:, p * LANES:(p + 1) * LANES] = cr[p]
    for hd in range(2 * n_pairs):
      r_ref[hd] = cr[n_pairs + hd]

  stash(state[2:])

  @pl.when((i >= 1) & (state[1] > ATTN_DEAD_LOG2))
  def _():
    stash(block(0, True,
                tuple(acc_ref[:, p * LANES:(p + 1) * LANES] for p in range(n_pairs))
                + tuple(r_ref[hd] for hd in range(2 * n_pairs))))

  o_ref[...] = acc_ref[...].astype(BF16)


def _attention(q, kv, *, batch, lp, pf):
  n, width = q.shape
  t = ATTN_TILE
  nq = lp // t
  bw = ATTN_PAIRS * LANES
  n_col = width // bw
  return pl.pallas_call(
      functools.partial(_attn_kernel, pf=pf),
      grid=(batch, n_col, nq),
      in_specs=[
          pl.BlockSpec((t, bw), lambda b, p, i: (b * nq + i, p)),
          pl.BlockSpec((lp, bw), lambda b, p, i: (b, p)),
          pl.BlockSpec((lp, bw), lambda b, p, i: (b, n_col + p)),
      ],
      out_specs=pl.BlockSpec((t, bw), lambda b, p, i: (b * nq + i, p)),
      out_shape=jax.ShapeDtypeStruct((n, width), BF16),
      scratch_shapes=[pltpu.VMEM((t, bw), F32),
                      pltpu.VMEM((2 * ATTN_PAIRS, t, 1), F32)],
      compiler_params=pltpu.CompilerParams(
          dimension_semantics=("arbitrary", "arbitrary", "arbitrary"),
          vmem_limit_bytes=VMEM_LIMIT),
      name="sb_attention",
  )(q, kv, kv)


def _row(v, width=None):
  v = v.astype(F32).reshape(1, -1)
  if width is not None and v.shape[1] < width:
    v = jnp.pad(v, ((0, 0), (0, width - v.shape[1])))
  return v


def kernel(x, meta_tokens, ssd_norm, ssd_w_in, ssd_conv_w, ssd_conv_b, ssd_dt_bias, ssd_a_log, ssd_d_skip, ssd_gate_norm, ssd_w_out, kv_norm, w_kv, sb_norm, sb_w_q, sb_w_o, ffn_norm, ffn_w_up, ffn_conv_w, ffn_conv_b, ffn_w_down, final_norm):
  batch, seq, d = x.shape
  depth = ffn_norm.shape[0]
  n_a = ssd_norm.shape[0]
  d_inner = ssd_w_out.shape[1]
  n_heads = ssd_a_log.shape[1]
  n_meta = meta_tokens.shape[0]
  length = n_meta + seq
  pf = (-n_meta) % SSM_CHUNK
  lp = pf + length
  assert all(lp % t == 0 for t in (SSM_CHUNK, IN_ROW_TILE, FFN_ROW_TILE,
                                   QKV_ROW_TILE, ATTN_TILE))
  assert n_heads <= LANES and d_inner == n_heads * SSM_HEAD_DIM
  dims = dict(batch=batch, lp=lp)

  h = jnp.concatenate([
      jnp.zeros((batch, pf, d), F32),
      jnp.broadcast_to(meta_tokens[None].astype(F32), (batch, n_meta, d)),
      x.astype(F32)], axis=1).reshape(batch * lp, d)

  expand = (jnp.arange(d_inner)[None, :] // SSM_HEAD_DIM
            == jnp.arange(LANES)[:, None]).astype(BF16)

  kv = None
  for layer in range(depth):
    attn = wo = None
    if layer < n_a:
      w_in = jnp.pad(ssd_w_in[layer].astype(BF16),
                     ((0, 0), (0, LANES - n_heads)))
      z, xbc, dt, bt, dtt = _ssd_in(
          h, _row(ssd_norm[layer]), w_in, ssd_conv_w[layer].astype(F32),
          _row(ssd_conv_b[layer]), _row(ssd_dt_bias[layer], LANES),
          pf=pf, n_heads=n_heads, n_z=d_inner, **dims)
      h = _ssd(xbc, bt, z, dt, dtt, h, _row(ssd_a_log[layer], LANES),
               ssd_a_log[layer].astype(F32).reshape(n_heads, 1),
               _row(jnp.repeat(ssd_d_skip[layer], SSM_HEAD_DIM)),
               _row(ssd_gate_norm[layer]), ssd_w_out[layer].astype(BF16),
               expand, **dims)
    else:
      j = layer - n_a
      if layer == n_a:
        kv, q = _qkv(h, _row(kv_norm), _row(sb_norm[j]), w_kv.astype(BF16),
                     sb_w_q[j].astype(BF16), q_scale=SB_HEAD_DIM ** -0.5 * LOG2_E)
      else:
        _, q = _qkv(h, _row(kv_norm), _row(sb_norm[j]), w_kv.astype(BF16),
                    sb_w_q[j].astype(BF16), q_scale=SB_HEAD_DIM ** -0.5 * LOG2_E)
      attn = _attention(q, kv, pf=pf, **dims)
      wo = sb_w_o[j].astype(BF16)
    fin_g = _row(final_norm) if layer == depth - 1 else None
    h = _ffn(h, attn, wo, _row(ffn_norm[layer]), ffn_w_up[layer].astype(BF16),
             ffn_conv_w[layer].astype(F32), _row(ffn_conv_b[layer]),
             ffn_w_down[layer].astype(BF16), fin_g, pf=pf, **dims)
  return h.reshape(batch, lp, d)[:, pf + n_meta:].astype(x.dtype)
```

```python
import functools

import jax
import jax.numpy as jnp
from jax import lax
from jax.experimental import pallas as pl
from jax.experimental.pallas import tpu as pltpu

F32 = jnp.float32
BF16 = jnp.bfloat16

N_META = 16
NORM_EPS = 1e-6
SSM_HEAD_DIM = 64
SSM_GROUPS = 4
SSM_STATE = 128
SSM_CHUNK = 256
SB_HEAD_DIM = 64
LOG2_E = 1.4426950408889634

LANES = 128
SUBLANES = 8
IN_ROW_TILE = 256
FFN_ROW_TILE = 256
QKV_ROW_TILE = 768
FFN_COL_TILE = 256
XBC_COL_TILE = 512
STAGE_SLOTS = 2
ATTN_TILE = 256
ATTN_PAIRS = 4
ATTN_DEAD_LOG2 = -160.0
VMEM_LIMIT = 56 * 1024 * 1024


def _resident(shape):
  nd = len(shape)
  return pl.BlockSpec(shape, lambda *_: (0,) * nd, pipeline_mode=pl.Buffered(1))


def _layer_resident(shape, layer):
  nd = len(shape)
  return pl.BlockSpec((None,) + tuple(shape[1:]),
                      lambda *_: (layer,) + (0,) * (nd - 1),
                      pipeline_mode=pl.Buffered(1))


def _stream_specs(stream, tm, nt):
  if isinstance(stream, tuple):
    head, body = stream
    assert head.shape[0] == tm
    return [_resident(head.shape),
            pl.BlockSpec((tm, body.shape[1]), lambda b, i: (
                b * (nt - 1) + jnp.maximum(i - 1, 0), 0))]
  return [pl.BlockSpec((tm, stream.shape[1]), lambda b, i: (b * nt + i, 0))]


def _stream_args(stream):
  return list(stream) if isinstance(stream, tuple) else [stream]


def _stream_tile(refs, i):
  if len(refs) == 2:
    return jnp.where(i == 0, refs[0][...], refs[1][...])
  return refs[0][...]


def _dot(a, b):
  return jnp.dot(a, b, preferred_element_type=F32)


def _split3(x):
  hi = x.astype(BF16)
  r1 = x - hi.astype(F32)
  mid = r1.astype(BF16)
  lo = (r1 - mid.astype(F32)).astype(BF16)
  return hi, mid, lo


def _dot_exact_rhs(x, m):
  hi, mid, lo = _split3(x)
  return _dot(hi, m) + _dot(mid, m) + _dot(lo, m)


def _dot_exact_lhs(m, x):
  hi, mid, lo = _split3(x)
  return _dot(m, hi) + _dot(m, mid) + _dot(m, lo)


def _rmsnorm(x, g):
  ms = jnp.mean(x * x, axis=-1, keepdims=True)
  return x * lax.rsqrt(ms + NORM_EPS) * g


def _sigmoid(x):
  return 1.0 / (1.0 + jnp.exp(-x))


def _softplus(x):
  return jnp.maximum(x, 0.0) + jnp.log(1.0 + jnp.exp(-jnp.abs(x)))


def _causal_conv(p, prev, w, b, stage_ref):
  tm = p.shape[0]
  width = w.shape[0]
  stage_ref[0:SUBLANES, :] = prev
  stage_ref[SUBLANES:SUBLANES + tm, :] = p
  y = p * w[width - 1:width, :] + b
  for k in range(1, width):
    y = y + (stage_ref[SUBLANES - k:SUBLANES - k + tm, :]
             * w[width - 1 - k:width - k, :])
  return y


def _ssd_in_kernel(*refs, pf, n_heads, b_off, n_stream):
  h_refs, refs = refs[:n_stream], refs[n_stream:]
  (g_ref, w_ref, cw_ref, cb_ref, dtb_ref, z_ref, xbc_ref, dt_ref, bt_ref,
   dtt_ref, carry_ref, stage_ref) = refs
  i = pl.program_id(1)
  tm = z_ref.shape[0]
  x_off = z_ref.shape[1]
  dt_off = x_off + xbc_ref.shape[1]
  assert b_off % XBC_COL_TILE == 0 and bt_ref.shape[0] == XBC_COL_TILE
  b_step = b_off // XBC_COL_TILE

  @pl.when(i == 0)
  def _():
    carry_ref[...] = jnp.zeros_like(carry_ref)

  u = _rmsnorm(_stream_tile(h_refs, i), g_ref[...]).astype(BF16)

  dt = _softplus(_dot(u, w_ref[:, dt_off:dt_off + LANES]) + dtb_ref[...])
  row = i * tm + lax.broadcasted_iota(jnp.int32, dt.shape, 0)
  lane = lax.broadcasted_iota(jnp.int32, dt.shape, 1)
  dt = jnp.where((row >= pf) & (lane < n_heads), dt, 0.0)
  dt_ref[...] = dt
  dtt_ref[...] = dt.T[0:dtt_ref.shape[0], :]

  n_steps = xbc_ref.shape[1] // XBC_COL_TILE
  cols = lambda c: slice(c * XBC_COL_TILE, (c + 1) * XBC_COL_TILE)
  n_z = z_ref.shape[1]
  assert n_z % XBC_COL_TILE == 0 and n_z // XBC_COL_TILE <= n_steps
  xcols = lambda c: slice(x_off + c * XBC_COL_TILE,
                          x_off + (c + 1) * XBC_COL_TILE)
  nxt = _dot(u, w_ref[:, xcols(0)])
  for c in range(n_steps):
    cs = cols(c)
    p = nxt
    if c + 1 < n_steps:
      nxt = _dot(u, w_ref[:, xcols(c + 1)])
    if c < n_z // XBC_COL_TILE:
      z_ref[:, cs] = _dot(u, w_ref[:, cs]).astype(BF16)
    y = _causal_conv(p, carry_ref[:, cs], cw_ref[:, cs], cb_ref[:, cs],
                     stage_ref.at[c % STAGE_SLOTS])
    carry_ref[:, cs] = p[tm - SUBLANES:tm, :]
    act = y * _sigmoid(y)
    xbc_ref[:, cs] = act.astype(BF16)
    if c == b_step:
      bt_ref[...] = act.T.astype(BF16)


def _ssd_in(h, g, w, cw, cb, dtb, *, batch, lp, pf, n_heads, n_z):
  n_x, n_dt = cw.shape[1], LANES
  assert w.shape[1] == n_z + n_x + n_dt
  tm = IN_ROW_TILE
  nt = lp // tm
  rows = lambda n: pl.BlockSpec((tm, n), lambda b, i: (b * nt + i, 0))
  cols = lambda n: pl.BlockSpec((None, n, tm), lambda b, i: (b, 0, i))
  n_b = SSM_GROUPS * SSM_STATE
  return pl.pallas_call(
      functools.partial(_ssd_in_kernel, pf=pf, n_heads=n_heads,
                        b_off=n_x - 2 * n_b, n_stream=len(_stream_args(h))),
      grid=(batch, nt),
      in_specs=_stream_specs(h, tm, nt) + [
          _resident(g.shape), _resident(w.shape), _resident(cw.shape),
          _resident(cb.shape), _resident(dtb.shape)],
      out_specs=[rows(n_z), rows(n_x), rows(n_dt), cols(n_b), cols(n_heads)],
      out_shape=[jax.ShapeDtypeStruct((batch * lp, n_z), BF16),
                 jax.ShapeDtypeStruct((batch * lp, n_x), BF16),
                 jax.ShapeDtypeStruct((batch * lp, n_dt), F32),
                 jax.ShapeDtypeStruct((batch, n_b, lp), BF16),
                 jax.ShapeDtypeStruct((batch, n_heads, lp), F32)],
      scratch_shapes=[
          pltpu.VMEM((SUBLANES, n_x), F32),
          pltpu.VMEM((STAGE_SLOTS, tm + SUBLANES, XBC_COL_TILE), F32)],
      compiler_params=pltpu.CompilerParams(
          dimension_semantics=("arbitrary", "arbitrary"),
          vmem_limit_bytes=VMEM_LIMIT),
      name="ssd_in",
  )(*_stream_args(h), g, w, cw, cb, dtb)


def _ssd_kernel(*refs, d_inner, n_stream):
  h_refs, refs = refs[:n_stream], refs[n_stream:]
  (xbc_ref, bt_ref, z_ref, dt_ref, dtt_ref, alog_row_ref, alog_col_ref,
   dskip_ref, gg_ref, wout_ref, expand_ref, o_ref, state_ref) = refs
  c = pl.program_id(1)
  q = o_ref.shape[0]
  n_state = SSM_STATE
  g_width = d_inner // SSM_GROUPS
  pairs_per_group = g_width // LANES

  @pl.when(c == 0)
  def _():
    state_ref[...] = jnp.zeros_like(state_ref)

  ri = lax.broadcasted_iota(jnp.int32, (q, q), 0)
  ci = lax.broadcasted_iota(jnp.int32, (q, q), 1)
  causal = ri >= ci
  lower = causal.astype(BF16)
  upper = (ri <= ci).astype(BF16)

  dt = dt_ref[...]
  acs = _dot_exact_lhs(lower, dt * -jnp.exp(alog_row_ref[...]))
  acs_t = _dot_exact_rhs(dtt_ref[...] * -jnp.exp(alog_col_ref[...]), upper)
  a_last = acs[q - 1:q, :]
  decay_to_end = jnp.exp(a_last - acs)
  decay_from_start = jnp.exp(acs)

  expand = expand_ref[...]
  dt_x = _dot(dt.astype(BF16), expand)
  dte_x = _dot(decay_to_end.astype(BF16), expand)
  dfs_x = _dot(decay_from_start.astype(BF16), expand)
  chunk_decay_x = _dot_exact_rhs(
      decay_from_start[q - SUBLANES:q, :], expand)[SUBLANES - 1:SUBLANES, :]

  xs = xbc_ref[:, 0:d_inner].astype(F32)
  xdt = xs * dt_x
  xdt_end = (xdt * dte_x).astype(BF16)
  xdt_b = xdt.astype(BF16)
  even_head = (lax.broadcasted_iota(jnp.int32, (1, d_inner), 1) % LANES
               ) < SSM_HEAD_DIM
  zero = jnp.zeros_like(xdt_b)
  xdt_even = jnp.where(even_head, xdt_b, zero)
  xdt_odd = jnp.where(even_head, zero, xdt_b)

  c_off = d_inner + SSM_GROUPS * n_state
  y_parts = []
  for g in range(SSM_GROUPS):
    gs = slice(g * g_width, (g + 1) * g_width)
    c_g = xbc_ref[:, c_off + g * n_state:c_off + (g + 1) * n_state]
    bt_g = bt_ref[g * n_state:(g + 1) * n_state, :]
    cb = _dot(c_g, bt_g)
    s_prev = state_ref[g]
    y_off = _dot(c_g, s_prev.astype(BF16)) * dfs_x[:, gs]
    state_ref[g] = s_prev * chunk_decay_x[:, gs] + _dot(bt_g, xdt_end[:, gs])
    pair_out = []
    for pr in range(pairs_per_group):
      cs = slice(g * g_width + pr * LANES, g * g_width + (pr + 1) * LANES)
      acc = None
      for half, xsrc in ((0, xdt_even), (1, xdt_odd)):
        hd = (g * g_width + pr * LANES) // SSM_HEAD_DIM + half
        diff = acs[:, hd:hd + 1] - acs_t[hd:hd + 1, :]
        decay = jnp.exp(jnp.where(causal, diff, -jnp.inf))
        part = _dot((cb * decay).astype(BF16), xsrc[:, cs])
        acc = part if acc is None else acc + part
      pair_out.append(acc)
    y_parts.append(jnp.concatenate(pair_out, axis=1) + y_off)

  zf = z_ref[...].astype(F32)
  gate = zf * _sigmoid(zf)
  normed = []
  for g in range(SSM_GROUPS):
    gs = slice(g * g_width, (g + 1) * g_width)
    hg = (y_parts[g] + xs[:, gs] * dskip_ref[:, gs]) * gate[:, gs]
    normed.append(_rmsnorm(hg, gg_ref[:, gs]).astype(BF16))
  hg_all = jnp.concatenate(normed, axis=1)
  o_ref[...] = _stream_tile(h_refs, c) + _dot(hg_all, wout_ref[...])


def _ssd(xbc, bt, z, dt, dtt, h, alog_row, alog_col, dskip_x, gate_g, wout,
         expand, *, batch, lp):
  d = wout.shape[1]
  d_inner = z.shape[1]
  q = SSM_CHUNK
  nc = lp // q
  n_heads = dtt.shape[1]
  rows = lambda n: pl.BlockSpec((q, n), lambda b, c: (b * nc + c, 0))
  cols = lambda n: pl.BlockSpec((None, n, q), lambda b, c: (b, 0, c))
  return pl.pallas_call(
      functools.partial(_ssd_kernel, d_inner=d_inner,
                        n_stream=len(_stream_args(h))),
      grid=(batch, nc),
      in_specs=_stream_specs(h, q, nc) + [
          rows(xbc.shape[1]), cols(bt.shape[1]), rows(d_inner),
          rows(dt.shape[1]), cols(n_heads),
          _resident(alog_row.shape), _resident(alog_col.shape),
          _resident(dskip_x.shape), _resident(gate_g.shape),
          _resident(wout.shape), _resident(expand.shape)],
      out_specs=rows(d),
      out_shape=jax.ShapeDtypeStruct((batch * lp, d), F32),
      scratch_shapes=[pltpu.VMEM(
          (SSM_GROUPS, SSM_STATE, d_inner // SSM_GROUPS), F32)],
      compiler_params=pltpu.CompilerParams(
          dimension_semantics=("arbitrary", "arbitrary"),
          vmem_limit_bytes=VMEM_LIMIT),
      name="ssd_scan",
  )(*_stream_args(h), xbc, bt, z, dt, dtt, alog_row, alog_col, dskip_x,
    gate_g, wout, expand)


def _ffn_kernel(*refs, pf, d_ff, has_attn, has_final):
  refs = list(refs)
  h_ref = refs.pop(0)
  if has_attn:
    attn_ref, wo_ref = refs.pop(0), refs.pop(0)
  g_ref, wup_ref, cw_ref, cb_ref, wdn_ref = (refs.pop(0) for _ in range(5))
  if has_final:
    fin_ref = refs.pop(0)
  o_ref, carry_ref, stage_ref = refs
  i = pl.program_id(1)
  tm = h_ref.shape[0]

  @pl.when(i == 0)
  def _():
    carry_ref[...] = jnp.zeros_like(carry_ref)

  h = h_ref[...]
  if has_attn:
    h = h + _dot(attn_ref[...], wo_ref[...])
  u = _rmsnorm(h, g_ref[...]).astype(BF16)

  def cols(c):
    return (slice(c * FFN_COL_TILE, (c + 1) * FFN_COL_TILE),
            slice(d_ff + c * FFN_COL_TILE, d_ff + (c + 1) * FFN_COL_TILE))

  def up(c):
    gs, vs = cols(c)
    return _dot(u, wup_ref[:, gs]), _dot(u, wup_ref[:, vs])

  n_steps = d_ff // FFN_COL_TILE
  acc = jnp.zeros(h.shape, F32)
  nxt = up(0)
  act = None
  for c in range(n_steps):
    gs, vs = cols(c)
    pg, pv = nxt
    if c + 1 < n_steps:
      nxt = up(c + 1)
    if act is not None:
      acc = acc + _dot(act, wdn_ref[cols(c - 1)[0], :])
    slot = 2 * (c % STAGE_SLOTS)
    yg = _causal_conv(pg, carry_ref[:, gs], cw_ref[:, gs], cb_ref[:, gs],
                      stage_ref.at[slot])
    yv = _causal_conv(pv, carry_ref[:, vs], cw_ref[:, vs], cb_ref[:, vs],
                      stage_ref.at[slot + 1])
    carry_ref[:, gs] = pg[tm - SUBLANES:tm, :]
    carry_ref[:, vs] = pv[tm - SUBLANES:tm, :]
    act = (yg * _sigmoid(yg) * yv).astype(BF16)
  acc = acc + _dot(act, wdn_ref[cols(n_steps - 1)[0], :])
  out = h + acc
  row = i * tm + lax.broadcasted_iota(jnp.int32, out.shape, 0)
  out = jnp.where(row >= pf, out, 0.0)
  if has_final:
    out = _rmsnorm(out, fin_ref[...])
  o_ref[...] = out


def _ffn(h, attn, wo, g, wup, cw, cb, wdn, fin_g, *, layer, batch, lp, pf,
         drop_rows):
  d = h.shape[1]
  d_ff = wdn.shape[1]
  tm = FFN_ROW_TILE
  nt = lp // tm
  rows = lambda n: pl.BlockSpec((tm, n), lambda b, i: (b * nt + i, 0))
  has_attn = attn is not None
  has_final = fin_g is not None
  args, specs = [h], [rows(d)]
  if has_attn:
    args += [attn, wo]
    specs += [rows(attn.shape[1]), _resident(wo.shape)]
  args += [g, wup, cw, cb, wdn]
  specs += [_layer_resident(a.shape, layer) for a in (g, wup, cw, cb, wdn)]
  if has_final:
    args.append(fin_g)
    specs.append(_resident(fin_g.shape))
    assert drop_rows == tm
    out_rows = lp - drop_rows
    out_spec = pl.BlockSpec(
        (tm, d), lambda b, i: (b * (nt - 1) + jnp.maximum(i - 1, 0), 0))
  else:
    out_rows = lp
    out_spec = rows(d)
  return pl.pallas_call(
      functools.partial(_ffn_kernel, pf=pf, d_ff=d_ff, has_attn=has_attn,
                        has_final=has_final),
      grid=(batch, nt),
      in_specs=specs,
      out_specs=out_spec,
      out_shape=jax.ShapeDtypeStruct((batch * out_rows, d), F32),
      scratch_shapes=[
          pltpu.VMEM((SUBLANES, 2 * d_ff), F32),
          pltpu.VMEM((2 * STAGE_SLOTS, tm + SUBLANES, FFN_COL_TILE), F32)],
      compiler_params=pltpu.CompilerParams(
          dimension_semantics=("arbitrary", "arbitrary"),
          vmem_limit_bytes=VMEM_LIMIT),
      name="conv_ffn",
  )(*args)


def _qkv_kernel(h_ref, gkv_ref, gq_ref, wkv_ref, wq_ref, kv_ref, q_ref, *,
                q_scale):
  h = h_ref[...]
  hn = h * lax.rsqrt(jnp.mean(h * h, axis=-1, keepdims=True) + NORM_EPS)
  kv_ref[...] = _dot((hn * gkv_ref[...]).astype(BF16), wkv_ref[...]).astype(BF16)
  q = _dot((hn * gq_ref[...]).astype(BF16), wq_ref[...])
  q_ref[...] = (q * q_scale).astype(BF16)


def _qkv(h, gkv, gq, wkv, wq, *, q_scale):
  n, d = h.shape
  rows = lambda m: pl.BlockSpec((QKV_ROW_TILE, m), lambda i: (i, 0))
  return pl.pallas_call(
      functools.partial(_qkv_kernel, q_scale=q_scale),
      grid=(n // QKV_ROW_TILE,),
      in_specs=[rows(d), _resident(gkv.shape), _resident(gq.shape),
                _resident(wkv.shape), _resident(wq.shape)],
      out_specs=[rows(wkv.shape[1]), rows(wq.shape[1])],
      out_shape=[jax.ShapeDtypeStruct((n, wkv.shape[1]), BF16),
                 jax.ShapeDtypeStruct((n, wq.shape[1]), BF16)],
      compiler_params=pltpu.CompilerParams(
          dimension_semantics=("arbitrary",), vmem_limit_bytes=VMEM_LIMIT),
      name="qkv_proj",
  )(h, gkv, gq, wkv, wq)


def _attn_kernel(q_ref, k_ref, v_ref, o_ref, acc_ref, r_ref, *, pf):
  i = pl.program_id(2)
  t = q_ref.shape[0]
  n_pairs = q_ref.shape[1] // LANES
  first_head = lax.broadcasted_iota(jnp.int32, (1, LANES), 1) < SB_HEAD_DIM
  q_heads = []
  for p in range(n_pairs):
    q = q_ref[:, p * LANES:(p + 1) * LANES]
    zq = jnp.zeros_like(q)
    q_heads += [jnp.where(first_head, q, zq), jnp.where(first_head, zq, q)]
  ri = lax.broadcasted_iota(jnp.int32, (t, t), 0)
  ci = lax.broadcasted_iota(jnp.int32, (t, t), 1)
  neg_suffix = jnp.where(ri >= ci, -1.0, 0.0).astype(BF16)

  def block(j, masked, carry):
    accs, r_heads = list(carry[:n_pairs]), carry[n_pairs:]
    start = pl.multiple_of(j * t, t)
    if masked:
      s_idx = j * t + ci
      visible = (s_idx < i * t + ri) & (s_idx >= pf)
    logits = []
    for hd, qa in enumerate(q_heads):
      p = hd // 2
      kb = k_ref[pl.ds(start, t), p * LANES:(p + 1) * LANES]
      logits.append(lax.dot_general(qa, kb, (((1,), (1,)), ((), ())),
                                    preferred_element_type=F32))
    suffix_sums = []
    for y in logits:
      sp = jnp.maximum(y, jnp.log2(1.0 + jnp.exp2(jnp.minimum(y, 126.0))))
      if masked:
        sp = jnp.where(visible, sp, 0.0)
      suffix_sums.append(_dot(sp.astype(BF16), neg_suffix))
    new_r = []
    for hd, (y, incl, ra) in enumerate(zip(logits, suffix_sums, r_heads)):
      p, half = divmod(hd, 2)
      vb = v_ref[pl.ds(start, t), p * LANES:(p + 1) * LANES]
      zv = jnp.zeros_like(vb)
      va = (jnp.where(first_head, vb, zv) if half == 0
            else jnp.where(first_head, zv, vb))
      arg = y + ra + incl
      if masked:
        arg = jnp.where(visible, arg, -jnp.inf)
      accs[p] = accs[p] + _dot(jnp.exp2(arg.astype(BF16)), va)
      new_r.append(ra + incl[:, 0:1])
    return (*accs, *new_r)

  def max_carried(cr):
    m = cr[n_pairs]
    for r in cr[n_pairs + 1:]:
      m = jnp.maximum(m, r)
    return jnp.max(m)

  def live(state):
    return (state[0] < jnp.maximum(i - 1, 0)) & (state[1] > ATTN_DEAD_LOG2)

  def step(state):
    out = block(i - 1 - state[0], False, state[2:])
    return (state[0] + 1, max_carried(out), *out)

  zero_r = jnp.zeros((t, 1), F32)
  carry = block(i, True, (jnp.zeros((t, LANES), F32),) * n_pairs
                + (zero_r,) * (2 * n_pairs))
  state = lax.while_loop(live, step, (jnp.int32(0), max_carried(carry), *carry))

  def stash(cr):
    for p in range(n_pairs):
      acc_ref[:, p * LANES:(p + 1) * LANES] = cr[p]
    for hd in range(2 * n_pairs):
      r_ref[hd] = cr[n_pairs + hd]

  stash(state[2:])

  @pl.when((i >= 1) & (state[1] > ATTN_DEAD_LOG2))
  def _():
    stash(block(0, True,
                tuple(acc_ref[:, p * LANES:(p + 1) * LANES] for p in range(n_pairs))
                + tuple(r_ref[hd] for hd in range(2 * n_pairs))))

  o_ref[...] = acc_ref[...].astype(BF16)


def _attention(q, kv, *, batch, lp, pf):
  n, width = q.shape
  t = ATTN_TILE
  nq = lp // t
  bw = ATTN_PAIRS * LANES
  n_col = width // bw
  return pl.pallas_call(
      functools.partial(_attn_kernel, pf=pf),
      grid=(batch, n_col, nq),
      in_specs=[
          pl.BlockSpec((t, bw), lambda b, p, i: (b * nq + i, p)),
          pl.BlockSpec((lp, bw), lambda b, p, i: (b, p)),
          pl.BlockSpec((lp, bw), lambda b, p, i: (b, n_col + p)),
      ],
      out_specs=pl.BlockSpec((t, bw), lambda b, p, i: (b * nq + i, p)),
      out_shape=jax.ShapeDtypeStruct((n, width), BF16),
      scratch_shapes=[pltpu.VMEM((t, bw), F32),
                      pltpu.VMEM((2 * ATTN_PAIRS, t, 1), F32)],
      compiler_params=pltpu.CompilerParams(
          dimension_semantics=("arbitrary", "arbitrary", "arbitrary"),
          vmem_limit_bytes=VMEM_LIMIT),
      name="sb_attention",
  )(q, kv, kv)


def _row(v, width=None):
  v = v.astype(F32).reshape(1, -1)
  if width is not None and v.shape[1] < width:
    v = jnp.pad(v, ((0, 0), (0, width - v.shape[1])))
  return v


def kernel(x, meta_tokens, ssd_norm, ssd_w_in, ssd_conv_w, ssd_conv_b, ssd_dt_bias, ssd_a_log, ssd_d_skip, ssd_gate_norm, ssd_w_out, kv_norm, w_kv, sb_norm, sb_w_q, sb_w_o, ffn_norm, ffn_w_up, ffn_conv_w, ffn_conv_b, ffn_w_down, final_norm):
  batch, seq, d = x.shape
  depth = ffn_norm.shape[0]
  n_a = ssd_norm.shape[0]
  d_inner = ssd_w_out.shape[1]
  n_heads = ssd_a_log.shape[1]
  n_meta = meta_tokens.shape[0]
  length = n_meta + seq
  pf = (-n_meta) % SSM_CHUNK
  lp = pf + length
  assert all(lp % t == 0 for t in (SSM_CHUNK, IN_ROW_TILE, FFN_ROW_TILE,
                                   QKV_ROW_TILE, ATTN_TILE))
  assert n_heads <= LANES and d_inner == n_heads * SSM_HEAD_DIM
  dims = dict(batch=batch, lp=lp)

  head = jnp.concatenate([jnp.zeros((pf, d), F32), meta_tokens.astype(F32)])
  h = (head, x.astype(F32).reshape(batch * seq, d))
  assert n_a >= 1 and depth > n_a

  ffn_g = ffn_norm.astype(F32).reshape(depth, 1, d)
  ffn_up = ffn_w_up.astype(BF16)
  ffn_cw = ffn_conv_w.astype(F32)
  ffn_cb = ffn_conv_b.astype(F32).reshape(depth, 1, -1)
  ffn_down = ffn_w_down.astype(BF16)

  expand = (jnp.arange(d_inner)[None, :] // SSM_HEAD_DIM
            == jnp.arange(LANES)[:, None]).astype(BF16)

  kv = None
  for layer in range(depth):
    attn = wo = None
    if layer < n_a:
      w_in = jnp.pad(ssd_w_in[layer].astype(BF16),
                     ((0, 0), (0, LANES - n_heads)))
      z, xbc, dt, bt, dtt = _ssd_in(
          h, _row(ssd_norm[layer]), w_in, ssd_conv_w[layer].astype(F32),
          _row(ssd_conv_b[layer]), _row(ssd_dt_bias[layer], LANES),
          pf=pf, n_heads=n_heads, n_z=d_inner, **dims)
      h = _ssd(xbc, bt, z, dt, dtt, h, _row(ssd_a_log[layer], LANES),
               ssd_a_log[layer].astype(F32).reshape(n_heads, 1),
               _row(jnp.repeat(ssd_d_skip[layer], SSM_HEAD_DIM)),
               _row(ssd_gate_norm[layer]), ssd_w_out[layer].astype(BF16),
               expand, **dims)
    else:
      j = layer - n_a
      if layer == n_a:
        kv, q = _qkv(h, _row(kv_norm), _row(sb_norm[j]), w_kv.astype(BF16),
                     sb_w_q[j].astype(BF16), q_scale=SB_HEAD_DIM ** -0.5 * LOG2_E)
      else:
        _, q = _qkv(h, _row(kv_norm), _row(sb_norm[j]), w_kv.astype(BF16),
                    sb_w_q[j].astype(BF16), q_scale=SB_HEAD_DIM ** -0.5 * LOG2_E)
      attn = _attention(q, kv, pf=pf, **dims)
      wo = sb_w_o[j].astype(BF16)
    fin_g = _row(final_norm) if layer == depth - 1 else None
    h = _ffn(h, attn, wo, ffn_g, ffn_up, ffn_cw, ffn_cb, ffn_down, fin_g,
             layer=layer, pf=pf, drop_rows=pf + n_meta, **dims)
  return h.reshape(batch, seq, d).astype(x.dtype)
```

```python
import functools

import jax
import jax.numpy as jnp
from jax import lax
from jax.experimental import pallas as pl
from jax.experimental.pallas import tpu as pltpu

F32 = jnp.float32
BF16 = jnp.bfloat16

N_META = 16
NORM_EPS = 1e-6
SSM_HEAD_DIM = 64
SSM_GROUPS = 4
SSM_STATE = 128
SSM_CHUNK = 256
SB_HEAD_DIM = 64
LOG2_E = 1.4426950408889634

LANES = 128
SUBLANES = 8
IN_ROW_TILE = 256
FFN_ROW_TILE = 256
QKV_ROW_TILE = 768
FFN_COL_TILE = 256
XBC_COL_TILE = 512
STAGE_SLOTS = 2
ATTN_TILE = 256
ATTN_PAIRS = 4
ATTN_DEAD_LOG2 = -160.0
VMEM_LIMIT = 56 * 1024 * 1024


def _resident(shape):
  nd = len(shape)
  return pl.BlockSpec(shape, lambda *_: (0,) * nd, pipeline_mode=pl.Buffered(1))


def _layer_resident(shape, layer):
  nd = len(shape)
  return pl.BlockSpec((None,) + tuple(shape[1:]),
                      lambda *_: (layer,) + (0,) * (nd - 1),
                      pipeline_mode=pl.Buffered(1))


def _stream_specs(stream, tm, nt):
  if isinstance(stream, tuple):
    head, body = stream
    assert head.shape[0] == tm
    return [_resident(head.shape),
            pl.BlockSpec((tm, body.shape[1]), lambda b, i: (
                b * (nt - 1) + jnp.maximum(i - 1, 0), 0))]
  return [pl.BlockSpec((tm, stream.shape[1]), lambda b, i: (b * nt + i, 0))]


def _stream_args(stream):
  return list(stream) if isinstance(stream, tuple) else [stream]


def _stream_tile(refs, i):
  if len(refs) == 2:
    return jnp.where(i == 0, refs[0][...], refs[1][...])
  return refs[0][...]


def _dot(a, b):
  return jnp.dot(a, b, preferred_element_type=F32)


def _split3(x):
  hi = x.astype(BF16)
  r1 = x - hi.astype(F32)
  mid = r1.astype(BF16)
  lo = (r1 - mid.astype(F32)).astype(BF16)
  return hi, mid, lo


def _dot_exact_rhs(x, m):
  hi, mid, lo = _split3(x)
  return _dot(hi, m) + _dot(mid, m) + _dot(lo, m)


def _dot_exact_lhs(m, x):
  hi, mid, lo = _split3(x)
  return _dot(m, hi) + _dot(m, mid) + _dot(m, lo)


def _rmsnorm(x, g):
  ms = jnp.mean(x * x, axis=-1, keepdims=True)
  return x * lax.rsqrt(ms + NORM_EPS) * g


def _sigmoid(x):
  return 1.0 / (1.0 + jnp.exp(-x))


def _softplus(x):
  return jnp.maximum(x, 0.0) + jnp.log(1.0 + jnp.exp(-jnp.abs(x)))


def _causal_conv(p, prev, w, b, stage_ref):
  tm = p.shape[0]
  width = w.shape[0]
  stage_ref[0:SUBLANES, :] = prev
  stage_ref[SUBLANES:SUBLANES + tm, :] = p
  y = p * w[width - 1:width, :] + b
  for k in range(1, width):
    y = y + (stage_ref[SUBLANES - k:SUBLANES - k + tm, :]
             * w[width - 1 - k:width - k, :])
  return y


def _ssd_in_kernel(*refs, pf, n_heads, b_off, n_stream):
  h_refs, refs = refs[:n_stream], refs[n_stream:]
  (g_ref, w_ref, cw_ref, cb_ref, dtb_ref, z_ref, xbc_ref, dt_ref, bt_ref,
   dtt_ref, carry_ref, stage_ref) = refs
  i = pl.program_id(1)
  tm = z_ref.shape[0]
  x_off = z_ref.shape[1]
  dt_off = x_off + xbc_ref.shape[1]
  assert b_off % XBC_COL_TILE == 0 and bt_ref.shape[0] == XBC_COL_TILE
  b_step = b_off // XBC_COL_TILE

  @pl.when(i == 0)
  def _():
    carry_ref[...] = jnp.zeros_like(carry_ref)

  u = _rmsnorm(_stream_tile(h_refs, i), g_ref[...]).astype(BF16)

  dt = _softplus(_dot(u, w_ref[:, dt_off:dt_off + LANES]) + dtb_ref[...])
  row = i * tm + lax.broadcasted_iota(jnp.int32, dt.shape, 0)
  lane = lax.broadcasted_iota(jnp.int32, dt.shape, 1)
  dt = jnp.where((row >= pf) & (lane < n_heads), dt, 0.0)
  dt_ref[...] = dt
  dtt_ref[...] = dt.T[0:dtt_ref.shape[0], :]

  n_steps = xbc_ref.shape[1] // XBC_COL_TILE
  cols = lambda c: slice(c * XBC_COL_TILE, (c + 1) * XBC_COL_TILE)
  n_z = z_ref.shape[1]
  assert n_z % XBC_COL_TILE == 0 and n_z // XBC_COL_TILE <= n_steps
  xcols = lambda c: slice(x_off + c * XBC_COL_TILE,
                          x_off + (c + 1) * XBC_COL_TILE)
  nxt = _dot(u, w_ref[:, xcols(0)])
  for c in range(n_steps):
    cs = cols(c)
    p = nxt
    if c + 1 < n_steps:
      nxt = _dot(u, w_ref[:, xcols(c + 1)])
    if c < n_z // XBC_COL_TILE:
      z_ref[:, cs] = _dot(u, w_ref[:, cs]).astype(BF16)
    y = _causal_conv(p, carry_ref[:, cs], cw_ref[:, cs], cb_ref[:, cs],
                     stage_ref.at[c % STAGE_SLOTS])
    carry_ref[:, cs] = p[tm - SUBLANES:tm, :]
    act = y * _sigmoid(y)
    xbc_ref[:, cs] = act.astype(BF16)
    if c == b_step:
      bt_ref[...] = act.T.astype(BF16)


def _ssd_in(h, g, w, cw, cb, dtb, *, batch, lp, pf, n_heads, n_z):
  n_x, n_dt = cw.shape[1], LANES
  assert w.shape[1] == n_z + n_x + n_dt
  tm = IN_ROW_TILE
  nt = lp // tm
  rows = lambda n: pl.BlockSpec((tm, n), lambda b, i: (b * nt + i, 0))
  cols = lambda n: pl.BlockSpec((None, n, tm), lambda b, i: (b, 0, i))
  n_b = SSM_GROUPS * SSM_STATE
  return pl.pallas_call(
      functools.partial(_ssd_in_kernel, pf=pf, n_heads=n_heads,
                        b_off=n_x - 2 * n_b, n_stream=len(_stream_args(h))),
      grid=(batch, nt),
      in_specs=_stream_specs(h, tm, nt) + [
          _resident(g.shape), _resident(w.shape), _resident(cw.shape),
          _resident(cb.shape), _resident(dtb.shape)],
      out_specs=[rows(n_z), rows(n_x), rows(n_dt), cols(n_b), cols(n_heads)],
      out_shape=[jax.ShapeDtypeStruct((batch * lp, n_z), BF16),
                 jax.ShapeDtypeStruct((batch * lp, n_x), BF16),
                 jax.ShapeDtypeStruct((batch * lp, n_dt), F32),
                 jax.ShapeDtypeStruct((batch, n_b, lp), BF16),
                 jax.ShapeDtypeStruct((batch, n_heads, lp), F32)],
      scratch_shapes=[
          pltpu.VMEM((SUBLANES, n_x), F32),
          pltpu.VMEM((STAGE_SLOTS, tm + SUBLANES, XBC_COL_TILE), F32)],
      compiler_params=pltpu.CompilerParams(
          dimension_semantics=("arbitrary", "arbitrary"),
          vmem_limit_bytes=VMEM_LIMIT),
      name="ssd_in",
  )(*_stream_args(h), g, w, cw, cb, dtb)


def _ssd_kernel(*refs, d_inner, n_stream):
  h_refs, refs = refs[:n_stream], refs[n_stream:]
  (xbc_ref, bt_ref, z_ref, dt_ref, dtt_ref, alog_row_ref, alog_col_ref,
   dskip_ref, gg_ref, wout_ref, expand_ref, o_ref, state_ref) = refs
  c = pl.program_id(1)
  q = o_ref.shape[0]
  n_state = SSM_STATE
  g_width = d_inner // SSM_GROUPS
  pairs_per_group = g_width // LANES

  @pl.when(c == 0)
  def _():
    state_ref[...] = jnp.zeros_like(state_ref)

  ri = lax.broadcasted_iota(jnp.int32, (q, q), 0)
  ci = lax.broadcasted_iota(jnp.int32, (q, q), 1)
  causal = ri >= ci
  lower = causal.astype(BF16)
  upper = (ri <= ci).astype(BF16)

  dt = dt_ref[...]
  acs = _dot_exact_lhs(lower, dt * -jnp.exp(alog_row_ref[...]))
  acs_t = _dot_exact_rhs(dtt_ref[...] * -jnp.exp(alog_col_ref[...]), upper)
  a_last = acs[q - 1:q, :]
  decay_to_end = jnp.exp(a_last - acs)
  decay_from_start = jnp.exp(acs)

  dt_b = dt.astype(BF16)
  dte_b = decay_to_end.astype(BF16)
  dfs_b = decay_from_start.astype(BF16)
  last_rows = decay_from_start[q - SUBLANES:q, :]
  even_head = (lax.broadcasted_iota(jnp.int32, (1, g_width), 1) % LANES
               ) < SSM_HEAD_DIM

  c_off = d_inner + SSM_GROUPS * n_state
  out = _stream_tile(h_refs, c)
  pending = None
  for g in range(SSM_GROUPS):
    gs = slice(g * g_width, (g + 1) * g_width)
    expand = expand_ref[:, gs]
    dt_x = _dot(dt_b, expand)
    dte_x = _dot(dte_b, expand)
    dfs_x = _dot(dfs_b, expand)
    chunk_decay_x = _dot_exact_rhs(last_rows, expand)[SUBLANES - 1:SUBLANES, :]

    xs = xbc_ref[:, gs].astype(F32)
    xdt = xs * dt_x
    xdt_end = (xdt * dte_x).astype(BF16)
    xdt_b = xdt.astype(BF16)
    zero = jnp.zeros_like(xdt_b)
    xdt_halves = (jnp.where(even_head, xdt_b, zero),
                  jnp.where(even_head, zero, xdt_b))

    c_g = xbc_ref[:, c_off + g * n_state:c_off + (g + 1) * n_state]
    bt_g = bt_ref[g * n_state:(g + 1) * n_state, :]
    cb = _dot(c_g, bt_g)
    s_prev = state_ref[g]
    y_off = _dot(c_g, s_prev.astype(BF16)) * dfs_x
    state_ref[g] = s_prev * chunk_decay_x + _dot(bt_g, xdt_end)
    if pending is not None:
      out = out + _dot(pending[0], wout_ref[pending[1], :])
    pair_out = []
    for pr in range(pairs_per_group):
      cs = slice(pr * LANES, (pr + 1) * LANES)
      acc = None
      for half, xsrc in enumerate(xdt_halves):
        hd = (g * g_width + pr * LANES) // SSM_HEAD_DIM + half
        diff = acs[:, hd:hd + 1] - acs_t[hd:hd + 1, :]
        decay = jnp.exp(jnp.where(causal, diff, -jnp.inf))
        part = _dot((cb * decay).astype(BF16), xsrc[:, cs])
        acc = part if acc is None else acc + part
      pair_out.append(acc)
    y = jnp.concatenate(pair_out, axis=1) + y_off
    zf = z_ref[:, gs].astype(F32)
    hg = (y + xs * dskip_ref[:, gs]) * (zf * _sigmoid(zf))
    pending = (_rmsnorm(hg, gg_ref[:, gs]).astype(BF16), gs)
  o_ref[...] = out + _dot(pending[0], wout_ref[pending[1], :])


def _ssd(xbc, bt, z, dt, dtt, h, alog_row, alog_col, dskip_x, gate_g, wout,
         expand, *, batch, lp):
  d = wout.shape[1]
  d_inner = z.shape[1]
  q = SSM_CHUNK
  nc = lp // q
  n_heads = dtt.shape[1]
  rows = lambda n: pl.BlockSpec((q, n), lambda b, c: (b * nc + c, 0))
  cols = lambda n: pl.BlockSpec((None, n, q), lambda b, c: (b, 0, c))
  return pl.pallas_call(
      functools.partial(_ssd_kernel, d_inner=d_inner,
                        n_stream=len(_stream_args(h))),
      grid=(batch, nc),
      in_specs=_stream_specs(h, q, nc) + [
          rows(xbc.shape[1]), cols(bt.shape[1]), rows(d_inner),
          rows(dt.shape[1]), cols(n_heads),
          _resident(alog_row.shape), _resident(alog_col.shape),
          _resident(dskip_x.shape), _resident(gate_g.shape),
          _resident(wout.shape), _resident(expand.shape)],
      out_specs=rows(d),
      out_shape=jax.ShapeDtypeStruct((batch * lp, d), F32),
      scratch_shapes=[pltpu.VMEM(
          (SSM_GROUPS, SSM_STATE, d_inner // SSM_GROUPS), F32)],
      compiler_params=pltpu.CompilerParams(
          dimension_semantics=("arbitrary", "arbitrary"),
          vmem_limit_bytes=VMEM_LIMIT),
      name="ssd_scan",
  )(*_stream_args(h), xbc, bt, z, dt, dtt, alog_row, alog_col, dskip_x,
    gate_g, wout, expand)


def _ffn_kernel(*refs, pf, d_ff, has_attn, has_final):
  refs = list(refs)
  h_ref = refs.pop(0)
  if has_attn:
    attn_ref, wo_ref = refs.pop(0), refs.pop(0)
  g_ref, wup_ref, cw_ref, cb_ref, wdn_ref = (refs.pop(0) for _ in range(5))
  if has_final:
    fin_ref = refs.pop(0)
  o_ref, carry_ref, stage_ref = refs
  i = pl.program_id(1)
  tm = h_ref.shape[0]

  @pl.when(i == 0)
  def _():
    carry_ref[...] = jnp.zeros_like(carry_ref)

  h = h_ref[...]
  if has_attn:
    h = h + _dot(attn_ref[...], wo_ref[...])
  u = _rmsnorm(h, g_ref[...]).astype(BF16)

  def cols(c):
    return (slice(c * FFN_COL_TILE, (c + 1) * FFN_COL_TILE),
            slice(d_ff + c * FFN_COL_TILE, d_ff + (c + 1) * FFN_COL_TILE))

  def up(c):
    gs, vs = cols(c)
    return _dot(u, wup_ref[:, gs]), _dot(u, wup_ref[:, vs])

  n_steps = d_ff // FFN_COL_TILE
  acc = jnp.zeros(h.shape, F32)
  nxt = up(0)
  act = None
  for c in range(n_steps):
    gs, vs = cols(c)
    pg, pv = nxt
    if c + 1 < n_steps:
      nxt = up(c + 1)
    if act is not None:
      acc = acc + _dot(act, wdn_ref[cols(c - 1)[0], :])
    slot = 2 * (c % STAGE_SLOTS)
    yg = _causal_conv(pg, carry_ref[:, gs], cw_ref[:, gs], cb_ref[:, gs],
                      stage_ref.at[slot])
    yv = _causal_conv(pv, carry_ref[:, vs], cw_ref[:, vs], cb_ref[:, vs],
                      stage_ref.at[slot + 1])
    carry_ref[:, gs] = pg[tm - SUBLANES:tm, :]
    carry_ref[:, vs] = pv[tm - SUBLANES:tm, :]
    act = (yg * _sigmoid(yg) * yv).astype(BF16)
  acc = acc + _dot(act, wdn_ref[cols(n_steps - 1)[0], :])
  out = h + acc
  row = i * tm + lax.broadcasted_iota(jnp.int32, out.shape, 0)
  out = jnp.where(row >= pf, out, 0.0)
  if has_final:
    out = _rmsnorm(out, fin_ref[...])
  o_ref[...] = out


def _ffn(h, attn, wo, g, wup, cw, cb, wdn, fin_g, *, layer, batch, lp, pf,
         drop_rows):
  d = h.shape[1]
  d_ff = wdn.shape[1]
  tm = FFN_ROW_TILE
  nt = lp // tm
  rows = lambda n: pl.BlockSpec((tm, n), lambda b, i: (b * nt + i, 0))
  has_attn = attn is not None
  has_final = fin_g is not None
  args, specs = [h], [rows(d)]
  if has_attn:
    args += [attn, wo]
    specs += [rows(attn.shape[1]), _resident(wo.shape)]
  args += [g, wup, cw, cb, wdn]
  specs += [_layer_resident(a.shape, layer) for a in (g, wup, cw, cb, wdn)]
  if has_final:
    args.append(fin_g)
    specs.append(_resident(fin_g.shape))
    assert drop_rows == tm
    out_rows = lp - drop_rows
    out_spec = pl.BlockSpec(
        (tm, d), lambda b, i: (b * (nt - 1) + jnp.maximum(i - 1, 0), 0))
  else:
    out_rows = lp
    out_spec = rows(d)
  return pl.pallas_call(
      functools.partial(_ffn_kernel, pf=pf, d_ff=d_ff, has_attn=has_attn,
                        has_final=has_final),
      grid=(batch, nt),
      in_specs=specs,
      out_specs=out_spec,
      out_shape=jax.ShapeDtypeStruct((batch * out_rows, d), F32),
      scratch_shapes=[
          pltpu.VMEM((SUBLANES, 2 * d_ff), F32),
          pltpu.VMEM((2 * STAGE_SLOTS, tm + SUBLANES, FFN_COL_TILE), F32)],
      compiler_params=pltpu.CompilerParams(
          dimension_semantics=("arbitrary", "arbitrary"),
          vmem_limit_bytes=VMEM_LIMIT),
      name="conv_ffn",
  )(*args)


def _qkv_kernel(h_ref, gkv_ref, gq_ref, wkv_ref, wq_ref, kv_ref, q_ref, *,
                q_scale):
  h = h_ref[...]
  hn = h * lax.rsqrt(jnp.mean(h * h, axis=-1, keepdims=True) + NORM_EPS)
  kv_ref[...] = _dot((hn * gkv_ref[...]).astype(BF16), wkv_ref[...]).astype(BF16)
  q = _dot((hn * gq_ref[...]).astype(BF16), wq_ref[...])
  q_ref[...] = (q * q_scale).astype(BF16)


def _qkv(h, gkv, gq, wkv, wq, *, q_scale):
  n, d = h.shape
  rows = lambda m: pl.BlockSpec((QKV_ROW_TILE, m), lambda i: (i, 0))
  return pl.pallas_call(
      functools.partial(_qkv_kernel, q_scale=q_scale),
      grid=(n // QKV_ROW_TILE,),
      in_specs=[rows(d), _resident(gkv.shape), _resident(gq.shape),
                _resident(wkv.shape), _resident(wq.shape)],
      out_specs=[rows(wkv.shape[1]), rows(wq.shape[1])],
      out_shape=[jax.ShapeDtypeStruct((n, wkv.shape[1]), BF16),
                 jax.ShapeDtypeStruct((n, wq.shape[1]), BF16)],
      compiler_params=pltpu.CompilerParams(
          dimension_semantics=("arbitrary",), vmem_limit_bytes=VMEM_LIMIT),
      name="qkv_proj",
  )(h, gkv, gq, wkv, wq)


def _attn_kernel(q_ref, k_ref, v_ref, o_ref, acc_ref, r_ref, *, pf):
  i = pl.program_id(2)
  t = q_ref.shape[0]
  n_pairs = q_ref.shape[1] // LANES
  first_head = lax.broadcasted_iota(jnp.int32, (1, LANES), 1) < SB_HEAD_DIM

  def block(j, masked, carry, row0=0, nr=t, nk=t):
    accs, r_heads = list(carry[:n_pairs]), carry[n_pairs:]
    start = pl.multiple_of(j * t, t)
    q_heads = []
    for p in range(n_pairs):
      q = q_ref[row0:row0 + nr, p * LANES:(p + 1) * LANES]
      zq = jnp.zeros_like(q)
      q_heads += [jnp.where(first_head, q, zq), jnp.where(first_head, zq, q)]
    neg_suffix = jnp.where(
        lax.broadcasted_iota(jnp.int32, (nk, nk), 0)
        >= lax.broadcasted_iota(jnp.int32, (nk, nk), 1), -1.0, 0.0).astype(BF16)
    if masked:
      s_idx = j * t + lax.broadcasted_iota(jnp.int32, (nr, nk), 1)
      t_idx = i * t + row0 + lax.broadcasted_iota(jnp.int32, (nr, nk), 0)
      visible = (s_idx < t_idx) & (s_idx >= pf)
    logits = []
    for hd, qa in enumerate(q_heads):
      p = hd // 2
      kb = k_ref[pl.ds(start, nk), p * LANES:(p + 1) * LANES]
      logits.append(lax.dot_general(qa, kb, (((1,), (1,)), ((), ())),
                                    preferred_element_type=F32))
    suffix_sums = []
    for y in logits:
      sp = jnp.maximum(y, jnp.log2(1.0 + jnp.exp2(jnp.minimum(y, 126.0))))
      if masked:
        sp = jnp.where(visible, sp, 0.0)
      suffix_sums.append(_dot(sp.astype(BF16), neg_suffix))
    new_r = []
    for hd, (y, incl, ra) in enumerate(zip(logits, suffix_sums, r_heads)):
      p, half = divmod(hd, 2)
      vb = v_ref[pl.ds(start, nk), p * LANES:(p + 1) * LANES]
      zv = jnp.zeros_like(vb)
      va = (jnp.where(first_head, vb, zv) if half == 0
            else jnp.where(first_head, zv, vb))
      arg = y + ra + incl
      if masked:
        arg = jnp.where(visible, arg, -jnp.inf)
      accs[p] = accs[p] + _dot(jnp.exp2(arg.astype(BF16)), va)
      new_r.append(ra + incl[:, 0:1])
    return (*accs, *new_r)

  def stash(cr, row0=0, nr=t):
    for p in range(n_pairs):
      acc_ref[row0:row0 + nr, p * LANES:(p + 1) * LANES] = cr[p]
    for hd in range(2 * n_pairs):
      r_ref[hd, row0:row0 + nr, :] = cr[n_pairs + hd]

  def stashed():
    return (tuple(acc_ref[:, p * LANES:(p + 1) * LANES] for p in range(n_pairs))
            + tuple(r_ref[hd] for hd in range(2 * n_pairs)))

  def diagonal(j):
    half = t // 2
    zeros = lambda nr: ((jnp.zeros((nr, LANES), F32),) * n_pairs
                        + (jnp.zeros((nr, 1), F32),) * (2 * n_pairs))
    stash(block(j, True, zeros(half), row0=0, nr=half, nk=half), 0, half)
    stash(block(j, True, zeros(t - half), row0=half, nr=t - half, nk=t),
          half, t - half)
    return stashed()

  def max_carried(cr):
    m = cr[n_pairs]
    for r in cr[n_pairs + 1:]:
      m = jnp.maximum(m, r)
    return jnp.max(m)

  def live(state):
    return (state[0] < jnp.maximum(i - 1, 0)) & (state[1] > ATTN_DEAD_LOG2)

  def step(state):
    out = block(i - 1 - state[0], False, state[2:])
    return (state[0] + 1, max_carried(out), *out)

  carry = diagonal(i)
  state = lax.while_loop(live, step, (jnp.int32(0), max_carried(carry), *carry))

  stash(state[2:])

  @pl.when((i >= 1) & (state[1] > ATTN_DEAD_LOG2))
  def _():
    stash(block(0, True, stashed()))

  o_ref[...] = acc_ref[...].astype(BF16)


def _attention(q, kv, *, batch, lp, pf):
  n, width = q.shape
  t = ATTN_TILE
  nq = lp // t
  bw = ATTN_PAIRS * LANES
  n_col = width // bw
  return pl.pallas_call(
      functools.partial(_attn_kernel, pf=pf),
      grid=(batch, n_col, nq),
      in_specs=[
          pl.BlockSpec((t, bw), lambda b, p, i: (b * nq + i, p)),
          pl.BlockSpec((lp, bw), lambda b, p, i: (b, p)),
          pl.BlockSpec((lp, bw), lambda b, p, i: (b, n_col + p)),
      ],
      out_specs=pl.BlockSpec((t, bw), lambda b, p, i: (b * nq + i, p)),
      out_shape=jax.ShapeDtypeStruct((n, width), BF16),
      scratch_shapes=[pltpu.VMEM((t, bw), F32),
                      pltpu.VMEM((2 * ATTN_PAIRS, t, 1), F32)],
      compiler_params=pltpu.CompilerParams(
          dimension_semantics=("arbitrary", "arbitrary", "arbitrary"),
          vmem_limit_bytes=VMEM_LIMIT),
      name="sb_attention",
  )(q, kv, kv)


def _row(v, width=None):
  v = v.astype(F32).reshape(1, -1)
  if width is not None and v.shape[1] < width:
    v = jnp.pad(v, ((0, 0), (0, width - v.shape[1])))
  return v


def kernel(x, meta_tokens, ssd_norm, ssd_w_in, ssd_conv_w, ssd_conv_b, ssd_dt_bias, ssd_a_log, ssd_d_skip, ssd_gate_norm, ssd_w_out, kv_norm, w_kv, sb_norm, sb_w_q, sb_w_o, ffn_norm, ffn_w_up, ffn_conv_w, ffn_conv_b, ffn_w_down, final_norm):
  batch, seq, d = x.shape
  depth = ffn_norm.shape[0]
  n_a = ssd_norm.shape[0]
  d_inner = ssd_w_out.shape[1]
  n_heads = ssd_a_log.shape[1]
  n_meta = meta_tokens.shape[0]
  length = n_meta + seq
  pf = (-n_meta) % SSM_CHUNK
  lp = pf + length
  assert all(lp % t == 0 for t in (SSM_CHUNK, IN_ROW_TILE, FFN_ROW_TILE,
                                   QKV_ROW_TILE, ATTN_TILE))
  assert n_heads <= LANES and d_inner == n_heads * SSM_HEAD_DIM
  dims = dict(batch=batch, lp=lp)

  head = jnp.concatenate([jnp.zeros((pf, d), F32), meta_tokens.astype(F32)])
  h = (head, x.astype(F32).reshape(batch * seq, d))
  assert n_a >= 1 and depth > n_a

  ffn_g = ffn_norm.astype(F32).reshape(depth, 1, d)
  ffn_up = ffn_w_up.astype(BF16)
  ffn_cw = ffn_conv_w.astype(F32)
  ffn_cb = ffn_conv_b.astype(F32).reshape(depth, 1, -1)
  ffn_down = ffn_w_down.astype(BF16)

  expand = (jnp.arange(d_inner)[None, :] // SSM_HEAD_DIM
            == jnp.arange(LANES)[:, None]).astype(BF16)

  kv = None
  for layer in range(depth):
    attn = wo = None
    if layer < n_a:
      w_in = jnp.pad(ssd_w_in[layer],
                     ((0, 0), (0, LANES - n_heads))).astype(BF16)
      z, xbc, dt, bt, dtt = _ssd_in(
          h, _row(ssd_norm[layer]), w_in, ssd_conv_w[layer].astype(F32),
          _row(ssd_conv_b[layer]), _row(ssd_dt_bias[layer], LANES),
          pf=pf, n_heads=n_heads, n_z=d_inner, **dims)
      h = _ssd(xbc, bt, z, dt, dtt, h, _row(ssd_a_log[layer], LANES),
               ssd_a_log[layer].astype(F32).reshape(n_heads, 1),
               _row(jnp.repeat(ssd_d_skip[layer], SSM_HEAD_DIM)),
               _row(ssd_gate_norm[layer]), ssd_w_out[layer].astype(BF16),
               expand, **dims)
    else:
      j = layer - n_a
      if layer == n_a:
        kv, q = _qkv(h, _row(kv_norm), _row(sb_norm[j]), w_kv.astype(BF16),
                     sb_w_q[j].astype(BF16), q_scale=SB_HEAD_DIM ** -0.5 * LOG2_E)
      else:
        _, q = _qkv(h, _row(kv_norm), _row(sb_norm[j]), w_kv.astype(BF16),
                    sb_w_q[j].astype(BF16), q_scale=SB_HEAD_DIM ** -0.5 * LOG2_E)
      attn = _attention(q, kv, pf=pf, **dims)
      wo = sb_w_o[j].astype(BF16)
    fin_g = _row(final_norm) if layer == depth - 1 else None
    h = _ffn(h, attn, wo, ffn_g, ffn_up, ffn_cw, ffn_cb, ffn_down, fin_g,
             layer=layer, pf=pf, drop_rows=pf + n_meta, **dims)
  return h.reshape(batch, seq, d).astype(x.dtype)
```

```python
import functools

import jax
import jax.numpy as jnp
from jax import lax
from jax.experimental import pallas as pl
from jax.experimental.pallas import tpu as pltpu

F32 = jnp.float32
BF16 = jnp.bfloat16

N_META = 16
NORM_EPS = 1e-6
SSM_HEAD_DIM = 64
SSM_GROUPS = 4
SSM_STATE = 128
SSM_CHUNK = 256
SB_HEAD_DIM = 64
LOG2_E = 1.4426950408889634

LANES = 128
SUBLANES = 8
IN_ROW_TILE = 256
FFN_ROW_TILE = 256
QKV_ROW_TILE = 768
FFN_COL_TILE = 256
XBC_COL_TILE = 256
STAGE_SLOTS = 2
ATTN_TILE = 256
ATTN_PAIRS = 4
ATTN_DEAD_LOG2 = -160.0
VMEM_LIMIT = 56 * 1024 * 1024


def _resident(shape):
  nd = len(shape)
  return pl.BlockSpec(shape, lambda *_: (0,) * nd, pipeline_mode=pl.Buffered(1))


def _layer_resident(shape, layer):
  nd = len(shape)
  return pl.BlockSpec((None,) + tuple(shape[1:]),
                      lambda *_: (layer,) + (0,) * (nd - 1),
                      pipeline_mode=pl.Buffered(1))


def _stream_specs(stream, tm, nt):
  if isinstance(stream, tuple):
    head, body = stream
    assert head.shape[0] == tm
    return [_resident(head.shape),
            pl.BlockSpec((tm, body.shape[1]), lambda b, i: (
                b * (nt - 1) + jnp.maximum(i - 1, 0), 0))]
  return [pl.BlockSpec((tm, stream.shape[1]), lambda b, i: (b * nt + i, 0))]


def _stream_args(stream):
  return list(stream) if isinstance(stream, tuple) else [stream]


def _stream_tile(refs, i):
  if len(refs) == 2:
    return jnp.where(i == 0, refs[0][...], refs[1][...])
  return refs[0][...]


def _dot(a, b):
  return jnp.dot(a, b, preferred_element_type=F32)


def _split3(x):
  hi = x.astype(BF16)
  r1 = x - hi.astype(F32)
  mid = r1.astype(BF16)
  lo = (r1 - mid.astype(F32)).astype(BF16)
  return hi, mid, lo


def _dot_exact_rhs(x, m):
  hi, mid, lo = _split3(x)
  return _dot(hi, m) + _dot(mid, m) + _dot(lo, m)


def _dot_exact_lhs(m, x):
  hi, mid, lo = _split3(x)
  return _dot(m, hi) + _dot(m, mid) + _dot(m, lo)


def _rmsnorm(x, g):
  ms = jnp.mean(x * x, axis=-1, keepdims=True)
  return x * lax.rsqrt(ms + NORM_EPS) * g


def _sigmoid(x):
  return 1.0 / (1.0 + jnp.exp(-x))


def _softplus(x):
  return jnp.maximum(x, 0.0) + jnp.log(1.0 + jnp.exp(-jnp.abs(x)))


def _causal_conv(p, prev, w, b, stage_ref):
  tm = p.shape[0]
  width = w.shape[0]
  stage_ref[0:SUBLANES, :] = prev
  stage_ref[SUBLANES:SUBLANES + tm, :] = p
  y = p * w[width - 1:width, :] + b
  for k in range(1, width):
    y = y + (stage_ref[SUBLANES - k:SUBLANES - k + tm, :]
             * w[width - 1 - k:width - k, :])
  return y


def _ssd_in_kernel(*refs, pf, n_heads, b_off, n_stream):
  h_refs, refs = refs[:n_stream], refs[n_stream:]
  (g_ref, w_ref, cw_ref, cb_ref, dtb_ref, z_ref, xbc_ref, dt_ref, bt_ref,
   dtt_ref, carry_ref, stage_ref) = refs
  i = pl.program_id(1)
  tm = z_ref.shape[0]
  x_off = z_ref.shape[1]
  dt_off = x_off + xbc_ref.shape[1]
  n_b = bt_ref.shape[0]
  assert b_off % XBC_COL_TILE == 0 and n_b % XBC_COL_TILE == 0

  @pl.when(i == 0)
  def _():
    carry_ref[...] = jnp.zeros_like(carry_ref)

  u = _rmsnorm(_stream_tile(h_refs, i), g_ref[...]).astype(BF16)

  dt = _softplus(_dot(u, w_ref[:, dt_off:dt_off + LANES]) + dtb_ref[...])
  row = i * tm + lax.broadcasted_iota(jnp.int32, dt.shape, 0)
  lane = lax.broadcasted_iota(jnp.int32, dt.shape, 1)
  dt = jnp.where((row >= pf) & (lane < n_heads), dt, 0.0)
  dt_ref[...] = dt
  dtt_ref[...] = dt.T[0:dtt_ref.shape[0], :]

  n_steps = xbc_ref.shape[1] // XBC_COL_TILE
  cols = lambda c: slice(c * XBC_COL_TILE, (c + 1) * XBC_COL_TILE)
  n_z = z_ref.shape[1]
  assert n_z % XBC_COL_TILE == 0 and n_z // XBC_COL_TILE <= n_steps
  xcols = lambda c: slice(x_off + c * XBC_COL_TILE,
                          x_off + (c + 1) * XBC_COL_TILE)
  nxt = _dot(u, w_ref[:, xcols(0)])
  for c in range(n_steps):
    cs = cols(c)
    p = nxt
    if c + 1 < n_steps:
      nxt = _dot(u, w_ref[:, xcols(c + 1)])
    if c < n_z // XBC_COL_TILE:
      z_ref[:, cs] = _dot(u, w_ref[:, cs]).astype(BF16)
    y = _causal_conv(p, carry_ref[:, cs], cw_ref[:, cs], cb_ref[:, cs],
                     stage_ref.at[c % STAGE_SLOTS])
    carry_ref[:, cs] = p[tm - SUBLANES:tm, :]
    act = y * _sigmoid(y)
    xbc_ref[:, cs] = act.astype(BF16)
    if b_off <= cs.start < b_off + n_b:
      bt_ref[cs.start - b_off:cs.stop - b_off, :] = act.T.astype(BF16)


def _ssd_in(h, g, w, cw, cb, dtb, *, batch, lp, pf, n_heads, n_z):
  n_x, n_dt = cw.shape[1], LANES
  assert w.shape[1] == n_z + n_x + n_dt
  tm = IN_ROW_TILE
  nt = lp // tm
  rows = lambda n: pl.BlockSpec((tm, n), lambda b, i: (b * nt + i, 0))
  cols = lambda n: pl.BlockSpec((None, n, tm), lambda b, i: (b, 0, i))
  n_b = SSM_GROUPS * SSM_STATE
  return pl.pallas_call(
      functools.partial(_ssd_in_kernel, pf=pf, n_heads=n_heads,
                        b_off=n_x - 2 * n_b, n_stream=len(_stream_args(h))),
      grid=(batch, nt),
      in_specs=_stream_specs(h, tm, nt) + [
          _resident(g.shape), _resident(w.shape), _resident(cw.shape),
          _resident(cb.shape), _resident(dtb.shape)],
      out_specs=[rows(n_z), rows(n_x), rows(n_dt), cols(n_b), cols(n_heads)],
      out_shape=[jax.ShapeDtypeStruct((batch * lp, n_z), BF16),
                 jax.ShapeDtypeStruct((batch * lp, n_x), BF16),
                 jax.ShapeDtypeStruct((batch * lp, n_dt), F32),
                 jax.ShapeDtypeStruct((batch, n_b, lp), BF16),
                 jax.ShapeDtypeStruct((batch, n_heads, lp), F32)],
      scratch_shapes=[
          pltpu.VMEM((SUBLANES, n_x), F32),
          pltpu.VMEM((STAGE_SLOTS, tm + SUBLANES, XBC_COL_TILE), F32)],
      compiler_params=pltpu.CompilerParams(
          dimension_semantics=("arbitrary", "arbitrary"),
          vmem_limit_bytes=VMEM_LIMIT),
      name="ssd_in",
  )(*_stream_args(h), g, w, cw, cb, dtb)


def _ssd_kernel(*refs, d_inner, n_stream):
  h_refs, refs = refs[:n_stream], refs[n_stream:]
  (xbc_ref, bt_ref, z_ref, dt_ref, dtt_ref, alog_row_ref, alog_col_ref,
   dskip_ref, gg_ref, wout_ref, expand_ref, o_ref, state_ref) = refs
  c = pl.program_id(1)
  q = o_ref.shape[0]
  n_state = SSM_STATE
  g_width = d_inner // SSM_GROUPS
  pairs_per_group = g_width // LANES

  @pl.when(c == 0)
  def _():
    state_ref[...] = jnp.zeros_like(state_ref)

  ri = lax.broadcasted_iota(jnp.int32, (q, q), 0)
  ci = lax.broadcasted_iota(jnp.int32, (q, q), 1)
  causal = ri >= ci
  lower = causal.astype(BF16)
  upper = (ri <= ci).astype(BF16)

  dt = dt_ref[...]
  acs = _dot_exact_lhs(lower, dt * -jnp.exp(alog_row_ref[...]))
  acs_t = _dot_exact_rhs(dtt_ref[...] * -jnp.exp(alog_col_ref[...]), upper)
  a_last = acs[q - 1:q, :]
  decay_to_end = jnp.exp(a_last - acs)
  decay_from_start = jnp.exp(acs)

  dt_b = dt.astype(BF16)
  dte_b = decay_to_end.astype(BF16)
  dfs_b = decay_from_start.astype(BF16)
  last_rows = decay_from_start[q - SUBLANES:q, :]
  even_head = (lax.broadcasted_iota(jnp.int32, (1, g_width), 1) % LANES
               ) < SSM_HEAD_DIM

  c_off = d_inner + SSM_GROUPS * n_state
  out = _stream_tile(h_refs, c)
  pending = None
  for g in range(SSM_GROUPS):
    gs = slice(g * g_width, (g + 1) * g_width)
    expand = expand_ref[:, gs]
    dt_x = _dot(dt_b, expand)
    dte_x = _dot(dte_b, expand)
    dfs_x = _dot(dfs_b, expand)
    chunk_decay_x = _dot_exact_rhs(last_rows, expand)[SUBLANES - 1:SUBLANES, :]

    xs = xbc_ref[:, gs].astype(F32)
    xdt = xs * dt_x
    xdt_end = (xdt * dte_x).astype(BF16)
    xdt_b = xdt.astype(BF16)
    zero = jnp.zeros_like(xdt_b)
    xdt_halves = (jnp.where(even_head, xdt_b, zero),
                  jnp.where(even_head, zero, xdt_b))

    c_g = xbc_ref[:, c_off + g * n_state:c_off + (g + 1) * n_state]
    bt_g = bt_ref[g * n_state:(g + 1) * n_state, :]
    cb = _dot(c_g, bt_g)
    s_prev = state_ref[g]
    y_off = _dot(c_g, s_prev.astype(BF16)) * dfs_x
    state_ref[g] = s_prev * chunk_decay_x + _dot(bt_g, xdt_end)
    if pending is not None:
      out = out + _dot(pending[0], wout_ref[pending[1], :])
    pair_out = []
    for pr in range(pairs_per_group):
      cs = slice(pr * LANES, (pr + 1) * LANES)
      acc = None
      for half, xsrc in enumerate(xdt_halves):
        hd = (g * g_width + pr * LANES) // SSM_HEAD_DIM + half
        diff = acs[:, hd:hd + 1] - acs_t[hd:hd + 1, :]
        decay = jnp.exp(jnp.where(causal, diff, -jnp.inf))
        part = _dot((cb * decay).astype(BF16), xsrc[:, cs])
        acc = part if acc is None else acc + part
      pair_out.append(acc)
    y = jnp.concatenate(pair_out, axis=1) + y_off
    zf = z_ref[:, gs].astype(F32)
    hg = (y + xs * dskip_ref[:, gs]) * (zf * _sigmoid(zf))
    pending = (_rmsnorm(hg, gg_ref[:, gs]).astype(BF16), gs)
  o_ref[...] = out + _dot(pending[0], wout_ref[pending[1], :])


def _ssd(xbc, bt, z, dt, dtt, h, alog_row, alog_col, dskip_x, gate_g, wout,
         expand, *, batch, lp):
  d = wout.shape[1]
  d_inner = z.shape[1]
  q = SSM_CHUNK
  nc = lp // q
  n_heads = dtt.shape[1]
  rows = lambda n: pl.BlockSpec((q, n), lambda b, c: (b * nc + c, 0))
  cols = lambda n: pl.BlockSpec((None, n, q), lambda b, c: (b, 0, c))
  return pl.pallas_call(
      functools.partial(_ssd_kernel, d_inner=d_inner,
                        n_stream=len(_stream_args(h))),
      grid=(batch, nc),
      in_specs=_stream_specs(h, q, nc) + [
          rows(xbc.shape[1]), cols(bt.shape[1]), rows(d_inner),
          rows(dt.shape[1]), cols(n_heads),
          _resident(alog_row.shape), _resident(alog_col.shape),
          _resident(dskip_x.shape), _resident(gate_g.shape),
          _resident(wout.shape), _resident(expand.shape)],
      out_specs=rows(d),
      out_shape=jax.ShapeDtypeStruct((batch * lp, d), F32),
      scratch_shapes=[pltpu.VMEM(
          (SSM_GROUPS, SSM_STATE, d_inner // SSM_GROUPS), F32)],
      compiler_params=pltpu.CompilerParams(
          dimension_semantics=("arbitrary", "arbitrary"),
          vmem_limit_bytes=VMEM_LIMIT),
      name="ssd_scan",
  )(*_stream_args(h), xbc, bt, z, dt, dtt, alog_row, alog_col, dskip_x,
    gate_g, wout, expand)


def _ffn_kernel(*refs, pf, d_ff, has_attn, has_final):
  refs = list(refs)
  h_ref = refs.pop(0)
  if has_attn:
    attn_ref, wo_ref = refs.pop(0), refs.pop(0)
  g_ref, wup_ref, cw_ref, cb_ref, wdn_ref = (refs.pop(0) for _ in range(5))
  if has_final:
    fin_ref = refs.pop(0)
  o_ref, carry_ref, stage_ref = refs
  i = pl.program_id(1)
  tm = h_ref.shape[0]

  @pl.when(i == 0)
  def _():
    carry_ref[...] = jnp.zeros_like(carry_ref)

  h = h_ref[...]
  if has_attn:
    h = h + _dot(attn_ref[...], wo_ref[...])
  u = _rmsnorm(h, g_ref[...]).astype(BF16)

  def cols(c):
    return (slice(c * FFN_COL_TILE, (c + 1) * FFN_COL_TILE),
            slice(d_ff + c * FFN_COL_TILE, d_ff + (c + 1) * FFN_COL_TILE))

  def up(c):
    gs, vs = cols(c)
    return _dot(u, wup_ref[:, gs]), _dot(u, wup_ref[:, vs])

  n_steps = d_ff // FFN_COL_TILE
  acc = jnp.zeros(h.shape, F32)
  nxt = up(0)
  act = None
  for c in range(n_steps):
    gs, vs = cols(c)
    pg, pv = nxt
    if c + 1 < n_steps:
      nxt = up(c + 1)
    if act is not None:
      acc = acc + _dot(act, wdn_ref[cols(c - 1)[0], :])
    slot = 2 * (c % STAGE_SLOTS)
    yg = _causal_conv(pg, carry_ref[:, gs], cw_ref[:, gs], cb_ref[:, gs],
                      stage_ref.at[slot])
    yv = _causal_conv(pv, carry_ref[:, vs], cw_ref[:, vs], cb_ref[:, vs],
                      stage_ref.at[slot + 1])
    carry_ref[:, gs] = pg[tm - SUBLANES:tm, :]
    carry_ref[:, vs] = pv[tm - SUBLANES:tm, :]
    act = (yg * _sigmoid(yg) * yv).astype(BF16)
  acc = acc + _dot(act, wdn_ref[cols(n_steps - 1)[0], :])
  out = h + acc
  row = i * tm + lax.broadcasted_iota(jnp.int32, out.shape, 0)
  out = jnp.where(row >= pf, out, 0.0)
  if has_final:
    out = _rmsnorm(out, fin_ref[...])
  o_ref[...] = out


def _ffn(h, attn, wo, g, wup, cw, cb, wdn, fin_g, *, layer, batch, lp, pf,
         drop_rows):
  d = h.shape[1]
  d_ff = wdn.shape[1]
  tm = FFN_ROW_TILE
  nt = lp // tm
  rows = lambda n: pl.BlockSpec((tm, n), lambda b, i: (b * nt + i, 0))
  has_attn = attn is not None
  has_final = fin_g is not None
  args, specs = [h], [rows(d)]
  if has_attn:
    args += [attn, wo]
    specs += [rows(attn.shape[1]), _resident(wo.shape)]
  args += [g, wup, cw, cb, wdn]
  specs += [_layer_resident(a.shape, layer) for a in (g, wup, cw, cb, wdn)]
  if has_final:
    args.append(fin_g)
    specs.append(_resident(fin_g.shape))
    assert drop_rows == tm
    out_rows = lp - drop_rows
    out_spec = pl.BlockSpec(
        (tm, d), lambda b, i: (b * (nt - 1) + jnp.maximum(i - 1, 0), 0))
  else:
    out_rows = lp
    out_spec = rows(d)
  return pl.pallas_call(
      functools.partial(_ffn_kernel, pf=pf, d_ff=d_ff, has_attn=has_attn,
                        has_final=has_final),
      grid=(batch, nt),
      in_specs=specs,
      out_specs=out_spec,
      out_shape=jax.ShapeDtypeStruct((batch * out_rows, d), F32),
      scratch_shapes=[
          pltpu.VMEM((SUBLANES, 2 * d_ff), F32),
          pltpu.VMEM((2 * STAGE_SLOTS, tm + SUBLANES, FFN_COL_TILE), F32)],
      compiler_params=pltpu.CompilerParams(
          dimension_semantics=("arbitrary", "arbitrary"),
          vmem_limit_bytes=VMEM_LIMIT),
      name="conv_ffn",
  )(*args)


def _qkv_kernel(h_ref, gkv_ref, gq_ref, wkv_ref, wq_ref, kv_ref, q_ref, *,
                q_scale):
  h = h_ref[...]
  hn = h * lax.rsqrt(jnp.mean(h * h, axis=-1, keepdims=True) + NORM_EPS)
  kv_ref[...] = _dot((hn * gkv_ref[...]).astype(BF16), wkv_ref[...]).astype(BF16)
  q = _dot((hn * gq_ref[...]).astype(BF16), wq_ref[...])
  q_ref[...] = (q * q_scale).astype(BF16)


def _qkv(h, gkv, gq, wkv, wq, *, q_scale):
  n, d = h.shape
  rows = lambda m: pl.BlockSpec((QKV_ROW_TILE, m), lambda i: (i, 0))
  return pl.pallas_call(
      functools.partial(_qkv_kernel, q_scale=q_scale),
      grid=(n // QKV_ROW_TILE,),
      in_specs=[rows(d), _resident(gkv.shape), _resident(gq.shape),
                _resident(wkv.shape), _resident(wq.shape)],
      out_specs=[rows(wkv.shape[1]), rows(wq.shape[1])],
      out_shape=[jax.ShapeDtypeStruct((n, wkv.shape[1]), BF16),
                 jax.ShapeDtypeStruct((n, wq.shape[1]), BF16)],
      compiler_params=pltpu.CompilerParams(
          dimension_semantics=("arbitrary",), vmem_limit_bytes=VMEM_LIMIT),
      name="qkv_proj",
  )(h, gkv, gq, wkv, wq)


def _attn_kernel(q_ref, k_ref, v_ref, o_ref, acc_ref, r_ref, *, pf):
  i = pl.program_id(2)
  t = q_ref.shape[0]
  n_pairs = q_ref.shape[1] // LANES
  first_head = lax.broadcasted_iota(jnp.int32, (1, LANES), 1) < SB_HEAD_DIM

  half = t // 2

  def block(j, masked, carry, row0=0, nr=t, nk=t, k0=0):
    assert not (masked and k0)
    accs, r_heads = list(carry[:n_pairs]), carry[n_pairs:]
    start = pl.multiple_of(j * t + k0, half)
    q_heads = []
    for p in range(n_pairs):
      q = q_ref[row0:row0 + nr, p * LANES:(p + 1) * LANES]
      zq = jnp.zeros_like(q)
      q_heads += [jnp.where(first_head, q, zq), jnp.where(first_head, zq, q)]
    neg_suffix = jnp.where(
        lax.broadcasted_iota(jnp.int32, (nk, nk), 0)
        >= lax.broadcasted_iota(jnp.int32, (nk, nk), 1), -1.0, 0.0).astype(BF16)
    if masked:
      s_idx = j * t + lax.broadcasted_iota(jnp.int32, (nr, nk), 1)
      t_idx = i * t + row0 + lax.broadcasted_iota(jnp.int32, (nr, nk), 0)
      visible = (s_idx < t_idx) & (s_idx >= pf)
    logits = []
    for hd, qa in enumerate(q_heads):
      p = hd // 2
      kb = k_ref[pl.ds(start, nk), p * LANES:(p + 1) * LANES]
      logits.append(lax.dot_general(qa, kb, (((1,), (1,)), ((), ())),
                                    preferred_element_type=F32))
    suffix_sums = []
    for y in logits:
      sp = jnp.maximum(y, jnp.log2(1.0 + jnp.exp2(jnp.minimum(y, 126.0))))
      if masked:
        sp = jnp.where(visible, sp, 0.0)
      suffix_sums.append(_dot(sp.astype(BF16), neg_suffix))
    new_r = []
    for hd, (y, incl, ra) in enumerate(zip(logits, suffix_sums, r_heads)):
      p, second = divmod(hd, 2)
      vb = v_ref[pl.ds(start, nk), p * LANES:(p + 1) * LANES]
      zv = jnp.zeros_like(vb)
      va = (jnp.where(first_head, zv, vb) if second
            else jnp.where(first_head, vb, zv))
      arg = y + ra + incl
      if masked:
        arg = jnp.where(visible, arg, -jnp.inf)
      accs[p] = accs[p] + _dot(jnp.exp2(arg.astype(BF16)), va)
      new_r.append(ra + incl[:, 0:1])
    return (*accs, *new_r)

  def stash(cr, row0=0, nr=t):
    for p in range(n_pairs):
      acc_ref[row0:row0 + nr, p * LANES:(p + 1) * LANES] = cr[p]
    for hd in range(2 * n_pairs):
      r_ref[hd, row0:row0 + nr, :] = cr[n_pairs + hd]

  def stashed():
    return (tuple(acc_ref[:, p * LANES:(p + 1) * LANES] for p in range(n_pairs))
            + tuple(r_ref[hd] for hd in range(2 * n_pairs)))

  def diagonal():
    zeros = ((jnp.zeros((half, LANES), F32),) * n_pairs
             + (jnp.zeros((half, 1), F32),) * (2 * n_pairs))
    return (block(i, True, zeros, row0=0, nr=half, nk=half),
            block(i, True, zeros, row0=half, nr=half, nk=t))

  def max_carried(cr):
    m = cr[n_pairs]
    for r in cr[n_pairs + 1:]:
      m = jnp.maximum(m, r)
    return jnp.max(m)

  n_loop = jnp.maximum(i - 2, 0)

  @pl.when(i >= 2)
  def _():
    upper, lower = diagonal()
    upper = block(i - 1, False, upper, row0=0, nr=half, nk=half, k0=half)
    lower = block(i - 1, False, lower, row0=half, nr=half, nk=half, k0=half)
    upper = block(i - 1, False, upper, row0=0, nr=half, nk=half, k0=0)
    stash(upper, 0, half)
    stash(lower, half, half)

  @pl.when(i < 2)
  def _():
    upper, lower = diagonal()
    stash(upper, 0, half)
    stash(lower, half, half)

  def lower_rows():
    return (tuple(acc_ref[half:t, p * LANES:(p + 1) * LANES]
                  for p in range(n_pairs))
            + tuple(r_ref[hd, half:t, :] for hd in range(2 * n_pairs)))

  @pl.when((i >= 2) & (max_carried(lower_rows()) > ATTN_DEAD_LOG2))
  def _():
    stash(block(i - 1, False, lower_rows(), row0=half, nr=half, nk=half, k0=0),
          half, half)

  def live(state):
    return (state[0] < n_loop) & (state[1] > ATTN_DEAD_LOG2)

  def step(state):
    out = block(i - 2 - state[0], False, state[2:])
    return (state[0] + 1, max_carried(out), *out)

  carry = stashed()
  state = lax.while_loop(live, step, (jnp.int32(0), max_carried(carry), *carry))

  stash(state[2:])

  @pl.when((i >= 1) & (state[1] > ATTN_DEAD_LOG2))
  def _():
    stash(block(0, True, stashed()))

  o_ref[...] = acc_ref[...].astype(BF16)


def _attention(q, kv, *, batch, lp, pf):
  n, width = q.shape
  t = ATTN_TILE
  nq = lp // t
  bw = ATTN_PAIRS * LANES
  n_col = width // bw
  return pl.pallas_call(
      functools.partial(_attn_kernel, pf=pf),
      grid=(batch, n_col, nq),
      in_specs=[
          pl.BlockSpec((t, bw), lambda b, p, i: (b * nq + i, p)),
          pl.BlockSpec((lp, bw), lambda b, p, i: (b, p)),
          pl.BlockSpec((lp, bw), lambda b, p, i: (b, n_col + p)),
      ],
      out_specs=pl.BlockSpec((t, bw), lambda b, p, i: (b * nq + i, p)),
      out_shape=jax.ShapeDtypeStruct((n, width), BF16),
      scratch_shapes=[pltpu.VMEM((t, bw), F32),
                      pltpu.VMEM((2 * ATTN_PAIRS, t, 1), F32)],
      compiler_params=pltpu.CompilerParams(
          dimension_semantics=("arbitrary", "arbitrary", "arbitrary"),
          vmem_limit_bytes=VMEM_LIMIT),
      name="sb_attention",
  )(q, kv, kv)


def _row(v, width=None):
  v = v.astype(F32).reshape(1, -1)
  if width is not None and v.shape[1] < width:
    v = jnp.pad(v, ((0, 0), (0, width - v.shape[1])))
  return v


def kernel(x, meta_tokens, ssd_norm, ssd_w_in, ssd_conv_w, ssd_conv_b, ssd_dt_bias, ssd_a_log, ssd_d_skip, ssd_gate_norm, ssd_w_out, kv_norm, w_kv, sb_norm, sb_w_q, sb_w_o, ffn_norm, ffn_w_up, ffn_conv_w, ffn_conv_b, ffn_w_down, final_norm):
  batch, seq, d = x.shape
  depth = ffn_norm.shape[0]
  n_a = ssd_norm.shape[0]
  d_inner = ssd_w_out.shape[1]
  n_heads = ssd_a_log.shape[1]
  n_meta = meta_tokens.shape[0]
  length = n_meta + seq
  pf = (-n_meta) % SSM_CHUNK
  lp = pf + length
  assert all(lp % t == 0 for t in (SSM_CHUNK, IN_ROW_TILE, FFN_ROW_TILE,
                                   QKV_ROW_TILE, ATTN_TILE))
  assert n_heads <= LANES and d_inner == n_heads * SSM_HEAD_DIM
  assert pf <= ATTN_TILE
  dims = dict(batch=batch, lp=lp)

  head = jnp.concatenate([jnp.zeros((pf, d), F32), meta_tokens.astype(F32)])
  h = (head, x.astype(F32).reshape(batch * seq, d))
  assert n_a >= 1 and depth > n_a

  ffn_g = ffn_norm.astype(F32).reshape(depth, 1, d)
  ffn_up = ffn_w_up.astype(BF16)
  ffn_cw = ffn_conv_w.astype(F32)
  ffn_cb = ffn_conv_b.astype(F32).reshape(depth, 1, -1)
  ffn_down = ffn_w_down.astype(BF16)

  expand = (jnp.arange(d_inner)[None, :] // SSM_HEAD_DIM
            == jnp.arange(LANES)[:, None]).astype(BF16)

  kv = None
  for layer in range(depth):
    attn = wo = None
    if layer < n_a:
      w_in = jnp.pad(ssd_w_in[layer].astype(BF16),
                     ((0, 0), (0, LANES - n_heads)))
      z, xbc, dt, bt, dtt = _ssd_in(
          h, _row(ssd_norm[layer]), w_in, ssd_conv_w[layer].astype(F32),
          _row(ssd_conv_b[layer]), _row(ssd_dt_bias[layer], LANES),
          pf=pf, n_heads=n_heads, n_z=d_inner, **dims)
      h = _ssd(xbc, bt, z, dt, dtt, h, _row(ssd_a_log[layer], LANES),
               ssd_a_log[layer].astype(F32).reshape(n_heads, 1),
               _row(jnp.repeat(ssd_d_skip[layer], SSM_HEAD_DIM)),
               _row(ssd_gate_norm[layer]), ssd_w_out[layer].astype(BF16),
               expand, **dims)
    else:
      j = layer - n_a
      if layer == n_a:
        kv, q = _qkv(h, _row(kv_norm), _row(sb_norm[j]), w_kv.astype(BF16),
                     sb_w_q[j].astype(BF16), q_scale=SB_HEAD_DIM ** -0.5 * LOG2_E)
      else:
        _, q = _qkv(h, _row(kv_norm), _row(sb_norm[j]), w_kv.astype(BF16),
                    sb_w_q[j].astype(BF16), q_scale=SB_HEAD_DIM ** -0.5 * LOG2_E)
      attn = _attention(q, kv, pf=pf, **dims)
      wo = sb_w_o[j].astype(BF16)
    fin_g = _row(final_norm) if layer == depth - 1 else None
    h = _ffn(h, attn, wo, ffn_g, ffn_up, ffn_cw, ffn_cb, ffn_down, fin_g,
             layer=layer, pf=pf, drop_rows=pf + n_meta, **dims)
  return h.reshape(batch, seq, d).astype(x.dtype)
```

```python
import functools

import jax
import jax.numpy as jnp
from jax import lax
from jax.experimental import pallas as pl
from jax.experimental.pallas import tpu as pltpu

F32 = jnp.float32
BF16 = jnp.bfloat16

N_META = 16
NORM_EPS = 1e-6
SSM_HEAD_DIM = 64
SSM_GROUPS = 4
SSM_STATE = 128
SSM_CHUNK = 256
SB_HEAD_DIM = 64
LOG2_E = 1.4426950408889634

LANES = 128
SUBLANES = 8
IN_ROW_TILE = 256
FFN_ROW_TILE = 256
QKV_ROW_TILE = 768
FFN_COL_TILE = 256
XBC_COL_TILE = 256
STAGE_SLOTS = 2
ATTN_TILE = 256
ATTN_PAIRS = 4
ATTN_DEAD_LOG2 = -160.0
VMEM_LIMIT = 56 * 1024 * 1024


def _resident(shape):
  nd = len(shape)
  return pl.BlockSpec(shape, lambda *_: (0,) * nd, pipeline_mode=pl.Buffered(1))


def _layer_resident(shape, layer):
  nd = len(shape)
  return pl.BlockSpec((None,) + tuple(shape[1:]),
                      lambda *_: (layer,) + (0,) * (nd - 1),
                      pipeline_mode=pl.Buffered(1))


def _stream_specs(stream, tm, nt):
  if isinstance(stream, tuple):
    head, body = stream
    assert head.shape[0] == tm
    return [_resident(head.shape),
            pl.BlockSpec((tm, body.shape[1]), lambda b, i: (
                b * (nt - 1) + jnp.maximum(i - 1, 0), 0))]
  return [pl.BlockSpec((tm, stream.shape[1]), lambda b, i: (b * nt + i, 0))]


def _stream_args(stream):
  return list(stream) if isinstance(stream, tuple) else [stream]


def _stream_tile(refs, i):
  if len(refs) == 2:
    return jnp.where(i == 0, refs[0][...], refs[1][...])
  return refs[0][...]


def _dot(a, b):
  return jnp.dot(a, b, preferred_element_type=F32)


def _split3(x):
  hi = x.astype(BF16)
  r1 = x - hi.astype(F32)
  mid = r1.astype(BF16)
  lo = (r1 - mid.astype(F32)).astype(BF16)
  return hi, mid, lo


def _dot_exact_rhs(x, m):
  hi, mid, lo = _split3(x)
  return _dot(hi, m) + _dot(mid, m) + _dot(lo, m)


def _dot_exact_lhs(m, x):
  hi, mid, lo = _split3(x)
  return _dot(m, hi) + _dot(m, mid) + _dot(m, lo)


def _rmsnorm(x, g):
  ms = jnp.mean(x * x, axis=-1, keepdims=True)
  return x * lax.rsqrt(ms + NORM_EPS) * g


def _sigmoid(x):
  return 1.0 / (1.0 + jnp.exp(-x))


def _softplus(x):
  return jnp.maximum(x, 0.0) + jnp.log(1.0 + jnp.exp(-jnp.abs(x)))


def _causal_conv(p, prev, w, b, stage_ref):
  tm = p.shape[0]
  width = w.shape[0]
  stage_ref[0:SUBLANES, :] = prev
  stage_ref[SUBLANES:SUBLANES + tm, :] = p
  y = p * w[width - 1:width, :] + b
  for k in range(1, width):
    y = y + (stage_ref[SUBLANES - k:SUBLANES - k + tm, :]
             * w[width - 1 - k:width - k, :])
  return y


def _ssd_in_kernel(*refs, pf, n_heads, b_off, n_stream):
  h_refs, refs = refs[:n_stream], refs[n_stream:]
  (g_ref, w_ref, cw_ref, cb_ref, dtb_ref, z_ref, xbc_ref, dt_ref, bt_ref,
   dtt_ref, carry_ref, stage_ref) = refs
  i = pl.program_id(1)
  tm = z_ref.shape[0]
  x_off = z_ref.shape[1]
  dt_off = x_off + xbc_ref.shape[1]
  n_b = bt_ref.shape[0]
  assert b_off % XBC_COL_TILE == 0 and n_b % XBC_COL_TILE == 0

  @pl.when(i == 0)
  def _():
    carry_ref[...] = jnp.zeros_like(carry_ref)

  u = _rmsnorm(_stream_tile(h_refs, i), g_ref[...]).astype(BF16)

  dt = _softplus(_dot(u, w_ref[:, dt_off:dt_off + LANES]) + dtb_ref[...])
  row = i * tm + lax.broadcasted_iota(jnp.int32, dt.shape, 0)
  lane = lax.broadcasted_iota(jnp.int32, dt.shape, 1)
  dt = jnp.where((row >= pf) & (lane < n_heads), dt, 0.0)
  dt_ref[...] = dt
  dtt_ref[...] = dt.T[0:dtt_ref.shape[0], :]

  n_steps = xbc_ref.shape[1] // XBC_COL_TILE
  cols = lambda c: slice(c * XBC_COL_TILE, (c + 1) * XBC_COL_TILE)
  n_z = z_ref.shape[1]
  assert n_z % XBC_COL_TILE == 0 and n_z // XBC_COL_TILE <= n_steps
  xcols = lambda c: slice(x_off + c * XBC_COL_TILE,
                          x_off + (c + 1) * XBC_COL_TILE)
  nxt = _dot(u, w_ref[:, xcols(0)])
  for c in range(n_steps):
    cs = cols(c)
    p = nxt
    if c + 1 < n_steps:
      nxt = _dot(u, w_ref[:, xcols(c + 1)])
    if c < n_z // XBC_COL_TILE:
      z_ref[:, cs] = _dot(u, w_ref[:, cs]).astype(BF16)
    y = _causal_conv(p, carry_ref[:, cs], cw_ref[:, cs], cb_ref[:, cs],
                     stage_ref.at[c % STAGE_SLOTS])
    carry_ref[:, cs] = p[tm - SUBLANES:tm, :]
    act = y * _sigmoid(y)
    xbc_ref[:, cs] = act.astype(BF16)
    if b_off <= cs.start < b_off + n_b:
      bt_ref[cs.start - b_off:cs.stop - b_off, :] = act.T.astype(BF16)


def _ssd_in(h, g, w, cw, cb, dtb, *, batch, lp, pf, n_heads, n_z):
  n_x, n_dt = cw.shape[1], LANES
  assert w.shape[1] == n_z + n_x + n_dt
  tm = IN_ROW_TILE
  nt = lp // tm
  rows = lambda n: pl.BlockSpec((tm, n), lambda b, i: (b * nt + i, 0))
  cols = lambda n: pl.BlockSpec((None, n, tm), lambda b, i: (b, 0, i))
  n_b = SSM_GROUPS * SSM_STATE
  return pl.pallas_call(
      functools.partial(_ssd_in_kernel, pf=pf, n_heads=n_heads,
                        b_off=n_x - 2 * n_b, n_stream=len(_stream_args(h))),
      grid=(batch, nt),
      in_specs=_stream_specs(h, tm, nt) + [
          _resident(g.shape), _resident(w.shape), _resident(cw.shape),
          _resident(cb.shape), _resident(dtb.shape)],
      out_specs=[rows(n_z), rows(n_x), rows(n_dt), cols(n_b), cols(n_heads)],
      out_shape=[jax.ShapeDtypeStruct((batch * lp, n_z), BF16),
                 jax.ShapeDtypeStruct((batch * lp, n_x), BF16),
                 jax.ShapeDtypeStruct((batch * lp, n_dt), F32),
                 jax.ShapeDtypeStruct((batch, n_b, lp), BF16),
                 jax.ShapeDtypeStruct((batch, n_heads, lp), F32)],
      scratch_shapes=[
          pltpu.VMEM((SUBLANES, n_x), F32),
          pltpu.VMEM((STAGE_SLOTS, tm + SUBLANES, XBC_COL_TILE), F32)],
      compiler_params=pltpu.CompilerParams(
          dimension_semantics=("arbitrary", "arbitrary"),
          vmem_limit_bytes=VMEM_LIMIT),
      name="ssd_in",
  )(*_stream_args(h), g, w, cw, cb, dtb)


def _ssd_kernel(*refs, d_inner, n_stream):
  h_refs, refs = refs[:n_stream], refs[n_stream:]
  (xbc_ref, bt_ref, z_ref, dt_ref, dtt_ref, alog_row_ref, alog_col_ref,
   dskip_ref, gg_ref, wout_ref, expand_ref, o_ref, state_ref) = refs
  c = pl.program_id(1)
  q = o_ref.shape[0]
  n_state = SSM_STATE
  g_width = d_inner // SSM_GROUPS
  pairs_per_group = g_width // LANES

  @pl.when(c == 0)
  def _():
    state_ref[...] = jnp.zeros_like(state_ref)

  ri = lax.broadcasted_iota(jnp.int32, (q, q), 0)
  ci = lax.broadcasted_iota(jnp.int32, (q, q), 1)
  causal = ri >= ci
  lower = causal.astype(BF16)
  upper = (ri <= ci).astype(BF16)

  dt = dt_ref[...]
  acs = _dot_exact_lhs(lower, dt * -jnp.exp(alog_row_ref[...]))
  acs_t = _dot_exact_rhs(dtt_ref[...] * -jnp.exp(alog_col_ref[...]), upper)
  a_last = acs[q - 1:q, :]
  decay_to_end = jnp.exp(a_last - acs)
  decay_from_start = jnp.exp(acs)

  dt_b = dt.astype(BF16)
  dte_b = decay_to_end.astype(BF16)
  dfs_b = decay_from_start.astype(BF16)
  last_rows = decay_from_start[q - SUBLANES:q, :]
  even_head = (lax.broadcasted_iota(jnp.int32, (1, g_width), 1) % LANES
               ) < SSM_HEAD_DIM

  c_off = d_inner + SSM_GROUPS * n_state
  out = _stream_tile(h_refs, c)
  pending = None
  for g in range(SSM_GROUPS):
    gs = slice(g * g_width, (g + 1) * g_width)
    expand = expand_ref[:, gs]
    dt_x = _dot(dt_b, expand)
    dte_x = _dot(dte_b, expand)
    dfs_x = _dot(dfs_b, expand)
    chunk_decay_x = _dot_exact_rhs(last_rows, expand)[SUBLANES - 1:SUBLANES, :]

    xs = xbc_ref[:, gs].astype(F32)
    xdt = xs * dt_x
    xdt_end = (xdt * dte_x).astype(BF16)
    xdt_b = xdt.astype(BF16)
    zero = jnp.zeros_like(xdt_b)
    xdt_halves = (jnp.where(even_head, xdt_b, zero),
                  jnp.where(even_head, zero, xdt_b))

    c_g = xbc_ref[:, c_off + g * n_state:c_off + (g + 1) * n_state]
    bt_g = bt_ref[g * n_state:(g + 1) * n_state, :]
    cb = _dot(c_g, bt_g)
    s_prev = state_ref[g]
    y_off = _dot(c_g, s_prev.astype(BF16)) * dfs_x
    state_ref[g] = s_prev * chunk_decay_x + _dot(bt_g, xdt_end)
    if pending is not None:
      out = out + _dot(pending[0], wout_ref[pending[1], :])
    pair_out = []
    for pr in range(pairs_per_group):
      cs = slice(pr * LANES, (pr + 1) * LANES)
      acc = None
      for half, xsrc in enumerate(xdt_halves):
        hd = (g * g_width + pr * LANES) // SSM_HEAD_DIM + half
        diff = acs[:, hd:hd + 1] - acs_t[hd:hd + 1, :]
        decay = jnp.exp(jnp.where(causal, diff, -jnp.inf))
        part = _dot((cb * decay).astype(BF16), xsrc[:, cs])
        acc = part if acc is None else acc + part
      pair_out.append(acc)
    y = jnp.concatenate(pair_out, axis=1) + y_off
    zf = z_ref[:, gs].astype(F32)
    hg = (y + xs * dskip_ref[:, gs]) * (zf * _sigmoid(zf))
    pending = (_rmsnorm(hg, gg_ref[:, gs]).astype(BF16), gs)
  o_ref[...] = out + _dot(pending[0], wout_ref[pending[1], :])


def _ssd(xbc, bt, z, dt, dtt, h, alog_row, alog_col, dskip_x, gate_g, wout,
         expand, *, batch, lp):
  d = wout.shape[1]
  d_inner = z.shape[1]
  q = SSM_CHUNK
  nc = lp // q
  n_heads = dtt.shape[1]
  rows = lambda n: pl.BlockSpec((q, n), lambda b, c: (b * nc + c, 0))
  cols = lambda n: pl.BlockSpec((None, n, q), lambda b, c: (b, 0, c))
  return pl.pallas_call(
      functools.partial(_ssd_kernel, d_inner=d_inner,
                        n_stream=len(_stream_args(h))),
      grid=(batch, nc),
      in_specs=_stream_specs(h, q, nc) + [
          rows(xbc.shape[1]), cols(bt.shape[1]), rows(d_inner),
          rows(dt.shape[1]), cols(n_heads),
          _resident(alog_row.shape), _resident(alog_col.shape),
          _resident(dskip_x.shape), _resident(gate_g.shape),
          _resident(wout.shape), _resident(expand.shape)],
      out_specs=rows(d),
      out_shape=jax.ShapeDtypeStruct((batch * lp, d), F32),
      scratch_shapes=[pltpu.VMEM(
          (SSM_GROUPS, SSM_STATE, d_inner // SSM_GROUPS), F32)],
      compiler_params=pltpu.CompilerParams(
          dimension_semantics=("arbitrary", "arbitrary"),
          vmem_limit_bytes=VMEM_LIMIT),
      name="ssd_scan",
  )(*_stream_args(h), xbc, bt, z, dt, dtt, alog_row, alog_col, dskip_x,
    gate_g, wout, expand)


def _ffn_kernel(*refs, pf, d_ff, has_attn, has_final):
  refs = list(refs)
  h_ref = refs.pop(0)
  if has_attn:
    attn_ref, wo_ref = refs.pop(0), refs.pop(0)
  g_ref, wup_ref, cw_ref, cb_ref, wdn_ref = (refs.pop(0) for _ in range(5))
  if has_final:
    fin_ref = refs.pop(0)
  o_ref, carry_ref, stage_ref = refs
  i = pl.program_id(1)
  tm = h_ref.shape[0]

  @pl.when(i == 0)
  def _():
    carry_ref[...] = jnp.zeros_like(carry_ref)

  h = h_ref[...]
  if has_attn:
    h = h + _dot(attn_ref[...], wo_ref[...])
  u = _rmsnorm(h, g_ref[...]).astype(BF16)

  def cols(c):
    return (slice(c * FFN_COL_TILE, (c + 1) * FFN_COL_TILE),
            slice(d_ff + c * FFN_COL_TILE, d_ff + (c + 1) * FFN_COL_TILE))

  def up(c):
    gs, vs = cols(c)
    return _dot(u, wup_ref[:, gs]), _dot(u, wup_ref[:, vs])

  n_steps = d_ff // FFN_COL_TILE
  acc = jnp.zeros(h.shape, F32)
  nxt = up(0)
  act = None
  for c in range(n_steps):
    gs, vs = cols(c)
    pg, pv = nxt
    if c + 1 < n_steps:
      nxt = up(c + 1)
    if act is not None:
      acc = acc + _dot(act, wdn_ref[cols(c - 1)[0], :])
    slot = 2 * (c % STAGE_SLOTS)
    yg = _causal_conv(pg, carry_ref[:, gs], cw_ref[:, gs], cb_ref[:, gs],
                      stage_ref.at[slot])
    yv = _causal_conv(pv, carry_ref[:, vs], cw_ref[:, vs], cb_ref[:, vs],
                      stage_ref.at[slot + 1])
    carry_ref[:, gs] = pg[tm - SUBLANES:tm, :]
    carry_ref[:, vs] = pv[tm - SUBLANES:tm, :]
    act = (yg * _sigmoid(yg) * yv).astype(BF16)
  acc = acc + _dot(act, wdn_ref[cols(n_steps - 1)[0], :])
  out = h + acc
  row = i * tm + lax.broadcasted_iota(jnp.int32, out.shape, 0)
  out = jnp.where(row >= pf, out, 0.0)
  if has_final:
    out = _rmsnorm(out, fin_ref[...])
  o_ref[...] = out


def _ffn(h, attn, wo, g, wup, cw, cb, wdn, fin_g, *, layer, batch, lp, pf,
         drop_rows):
  d = h.shape[1]
  d_ff = wdn.shape[1]
  tm = FFN_ROW_TILE
  nt = lp // tm
  rows = lambda n: pl.BlockSpec((tm, n), lambda b, i: (b * nt + i, 0))
  has_attn = attn is not None
  has_final = fin_g is not None
  args, specs = [h], [rows(d)]
  if has_attn:
    args += [attn, wo]
    specs += [rows(attn.shape[1]), _resident(wo.shape)]
  args += [g, wup, cw, cb, wdn]
  specs += [_layer_resident(a.shape, layer) for a in (g, wup, cw, cb, wdn)]
  if has_final:
    args.append(fin_g)
    specs.append(_resident(fin_g.shape))
    assert drop_rows == tm
    out_rows = lp - drop_rows
    out_spec = pl.BlockSpec(
        (tm, d), lambda b, i: (b * (nt - 1) + jnp.maximum(i - 1, 0), 0))
  else:
    out_rows = lp
    out_spec = rows(d)
  return pl.pallas_call(
      functools.partial(_ffn_kernel, pf=pf, d_ff=d_ff, has_attn=has_attn,
                        has_final=has_final),
      grid=(batch, nt),
      in_specs=specs,
      out_specs=out_spec,
      out_shape=jax.ShapeDtypeStruct((batch * out_rows, d), F32),
      scratch_shapes=[
          pltpu.VMEM((SUBLANES, 2 * d_ff), F32),
          pltpu.VMEM((2 * STAGE_SLOTS, tm + SUBLANES, FFN_COL_TILE), F32)],
      compiler_params=pltpu.CompilerParams(
          dimension_semantics=("arbitrary", "arbitrary"),
          vmem_limit_bytes=VMEM_LIMIT),
      name="conv_ffn",
  )(*args)


def _qkv_kernel(h_ref, gkv_ref, gq_ref, wkv_ref, wq_ref, kv_ref, q_ref, *,
                q_scale):
  h = h_ref[...]
  hn = h * lax.rsqrt(jnp.mean(h * h, axis=-1, keepdims=True) + NORM_EPS)
  kv_ref[...] = _dot((hn * gkv_ref[...]).astype(BF16), wkv_ref[...]).astype(BF16)
  q = _dot((hn * gq_ref[...]).astype(BF16), wq_ref[...])
  q_ref[...] = (q * q_scale).astype(BF16)


def _qkv(h, gkv, gq, wkv, wq, *, q_scale):
  n, d = h.shape
  rows = lambda m: pl.BlockSpec((QKV_ROW_TILE, m), lambda i: (i, 0))
  return pl.pallas_call(
      functools.partial(_qkv_kernel, q_scale=q_scale),
      grid=(n // QKV_ROW_TILE,),
      in_specs=[rows(d), _resident(gkv.shape), _resident(gq.shape),
                _resident(wkv.shape), _resident(wq.shape)],
      out_specs=[rows(wkv.shape[1]), rows(wq.shape[1])],
      out_shape=[jax.ShapeDtypeStruct((n, wkv.shape[1]), BF16),
                 jax.ShapeDtypeStruct((n, wq.shape[1]), BF16)],
      compiler_params=pltpu.CompilerParams(
          dimension_semantics=("arbitrary",), vmem_limit_bytes=VMEM_LIMIT),
      name="qkv_proj",
  )(h, gkv, gq, wkv, wq)


def _attn_kernel(q_ref, k_ref, v_ref, o_ref, acc_ref, r_ref, live_ref, *, pf):
  i = pl.program_id(2)
  t = q_ref.shape[0]
  n_pairs = q_ref.shape[1] // LANES
  first_head = lax.broadcasted_iota(jnp.int32, (1, LANES), 1) < SB_HEAD_DIM

  half = t // 2

  def block(j, masked, carry, row0=0, nr=t, nk=t, k0=0, has_padding=True):
    assert not (masked and k0)
    accs, r_heads = list(carry[:n_pairs]), carry[n_pairs:]
    start = pl.multiple_of(j * t + k0, half)
    q_heads = []
    for p in range(n_pairs):
      q = q_ref[row0:row0 + nr, p * LANES:(p + 1) * LANES]
      zq = jnp.zeros_like(q)
      q_heads += [jnp.where(first_head, q, zq), jnp.where(first_head, zq, q)]
    neg_suffix = jnp.where(
        lax.broadcasted_iota(jnp.int32, (nk, nk), 0)
        >= lax.broadcasted_iota(jnp.int32, (nk, nk), 1), -1.0, 0.0).astype(BF16)
    if masked:
      s_idx = j * t + lax.broadcasted_iota(jnp.int32, (nr, nk), 1)
      t_idx = i * t + row0 + lax.broadcasted_iota(jnp.int32, (nr, nk), 0)
      visible = s_idx < t_idx
      if has_padding:
        visible = visible & (s_idx >= pf)
    logits = []
    for hd, qa in enumerate(q_heads):
      p = hd // 2
      kb = k_ref[pl.ds(start, nk), p * LANES:(p + 1) * LANES]
      logits.append(lax.dot_general(qa, kb, (((1,), (1,)), ((), ())),
                                    preferred_element_type=F32))
    suffix_sums = []
    for y in logits:
      sp = jnp.maximum(y, jnp.log2(1.0 + jnp.exp2(jnp.minimum(y, 126.0))))
      if masked:
        sp = jnp.where(visible, sp, 0.0)
      suffix_sums.append(_dot(sp.astype(BF16), neg_suffix))
    new_r = []
    for hd, (y, incl, ra) in enumerate(zip(logits, suffix_sums, r_heads)):
      p, second = divmod(hd, 2)
      vb = v_ref[pl.ds(start, nk), p * LANES:(p + 1) * LANES]
      zv = jnp.zeros_like(vb)
      va = (jnp.where(first_head, zv, vb) if second
            else jnp.where(first_head, vb, zv))
      arg = y + ra + incl
      if masked:
        arg = jnp.where(visible, arg, -jnp.inf)
      accs[p] = accs[p] + _dot(jnp.exp2(arg.astype(BF16)), va)
      new_r.append(ra + incl[:, 0:1])
    return (*accs, *new_r)

  def stash(cr, row0=0, nr=t):
    for p in range(n_pairs):
      acc_ref[row0:row0 + nr, p * LANES:(p + 1) * LANES] = cr[p]
    for hd in range(2 * n_pairs):
      r_ref[hd, row0:row0 + nr, :] = cr[n_pairs + hd]

  def stashed():
    return (tuple(acc_ref[:, p * LANES:(p + 1) * LANES] for p in range(n_pairs))
            + tuple(r_ref[hd] for hd in range(2 * n_pairs)))

  def diagonal(has_padding):
    zeros = ((jnp.zeros((half, LANES), F32),) * n_pairs
             + (jnp.zeros((half, 1), F32),) * (2 * n_pairs))
    return (block(i, True, zeros, row0=0, nr=half, nk=half,
                  has_padding=has_padding),
            block(i, True, zeros, row0=half, nr=half, nk=t,
                  has_padding=has_padding))

  def max_carried(cr):
    m = cr[n_pairs]
    for r in cr[n_pairs + 1:]:
      m = jnp.maximum(m, r)
    return jnp.max(m)

  n_loop = jnp.maximum(i - 2, 0)

  @pl.when(i >= 2)
  def _():
    upper, lower = diagonal(has_padding=False)
    upper = block(i - 1, False, upper, row0=0, nr=half, nk=half, k0=half)
    lower = block(i - 1, False, lower, row0=half, nr=half, nk=half, k0=half)
    upper = block(i - 1, False, upper, row0=0, nr=half, nk=half, k0=0)
    stash(upper, 0, half)
    stash(lower, half, half)

  @pl.when(i < 2)
  def _():
    upper, lower = diagonal(has_padding=True)
    stash(upper, 0, half)
    stash(lower, half, half)

  def lower_rows():
    return (tuple(acc_ref[half:t, p * LANES:(p + 1) * LANES]
                  for p in range(n_pairs))
            + tuple(r_ref[hd, half:t, :] for hd in range(2 * n_pairs)))

  def rows_max(lo, hi):
    m = r_ref[0, lo:hi, :]
    for hd in range(1, 2 * n_pairs):
      m = jnp.maximum(m, r_ref[hd, lo:hi, :])
    return jnp.max(m)

  max_upper, max_lower = rows_max(0, half), rows_max(half, t)
  live_ref[0] = jnp.maximum(max_upper, max_lower)

  @pl.when((i >= 2) & (max_lower > ATTN_DEAD_LOG2))
  def _():
    out = block(i - 1, False, lower_rows(), row0=half, nr=half, nk=half, k0=0)
    stash(out, half, half)
    live_ref[0] = jnp.maximum(max_upper, max_carried(out))

  def live(state):
    return (state[0] < n_loop) & (state[1] > ATTN_DEAD_LOG2)

  def step(state):
    out = block(i - 2 - state[0], False, state[2:])
    return (state[0] + 1, max_carried(out), *out)

  @pl.when((n_loop > 0) & (live_ref[0] > ATTN_DEAD_LOG2))
  def _():
    state = lax.while_loop(live, step, (jnp.int32(0), live_ref[0], *stashed()))
    stash(state[2:])
    live_ref[0] = state[1]

  @pl.when((i >= 1) & (live_ref[0] > ATTN_DEAD_LOG2))
  def _():
    stash(block(0, True, stashed()))

  o_ref[...] = acc_ref[...].astype(BF16)


def _attention(q, kv, *, batch, lp, pf):
  n, width = q.shape
  t = ATTN_TILE
  nq = lp // t
  bw = ATTN_PAIRS * LANES
  n_col = width // bw
  return pl.pallas_call(
      functools.partial(_attn_kernel, pf=pf),
      grid=(batch, n_col, nq),
      in_specs=[
          pl.BlockSpec((t, bw), lambda b, p, i: (b * nq + i, p)),
          pl.BlockSpec((lp, bw), lambda b, p, i: (b, p)),
          pl.BlockSpec((lp, bw), lambda b, p, i: (b, n_col + p)),
      ],
      out_specs=pl.BlockSpec((t, bw), lambda b, p, i: (b * nq + i, p)),
      out_shape=jax.ShapeDtypeStruct((n, width), BF16),
      scratch_shapes=[pltpu.VMEM((t, bw), F32),
                      pltpu.VMEM((2 * ATTN_PAIRS, t, 1), F32),
                      pltpu.SMEM((1,), F32)],
      compiler_params=pltpu.CompilerParams(
          dimension_semantics=("arbitrary", "arbitrary", "arbitrary"),
          vmem_limit_bytes=VMEM_LIMIT),
      name="sb_attention",
  )(q, kv, kv)


def _row(v, width=None):
  v = v.astype(F32).reshape(1, -1)
  if width is not None and v.shape[1] < width:
    v = jnp.pad(v, ((0, 0), (0, width - v.shape[1])))
  return v


def kernel(x, meta_tokens, ssd_norm, ssd_w_in, ssd_conv_w, ssd_conv_b, ssd_dt_bias, ssd_a_log, ssd_d_skip, ssd_gate_norm, ssd_w_out, kv_norm, w_kv, sb_norm, sb_w_q, sb_w_o, ffn_norm, ffn_w_up, ffn_conv_w, ffn_conv_b, ffn_w_down, final_norm):
  batch, seq, d = x.shape
  depth = ffn_norm.shape[0]
  n_a = ssd_norm.shape[0]
  d_inner = ssd_w_out.shape[1]
  n_heads = ssd_a_log.shape[1]
  n_meta = meta_tokens.shape[0]
  length = n_meta + seq
  pf = (-n_meta) % SSM_CHUNK
  lp = pf + length
  assert all(lp % t == 0 for t in (SSM_CHUNK, IN_ROW_TILE, FFN_ROW_TILE,
                                   QKV_ROW_TILE, ATTN_TILE))
  assert n_heads <= LANES and d_inner == n_heads * SSM_HEAD_DIM
  assert pf <= ATTN_TILE
  dims = dict(batch=batch, lp=lp)

  head = jnp.concatenate([jnp.zeros((pf, d), F32), meta_tokens.astype(F32)])
  h = (head, x.astype(F32).reshape(batch * seq, d))
  assert n_a >= 1 and depth > n_a

  ffn_g = ffn_norm.astype(F32).reshape(depth, 1, d)
  ffn_up = ffn_w_up.astype(BF16)
  ffn_cw = ffn_conv_w.astype(F32)
  ffn_cb = ffn_conv_b.astype(F32).reshape(depth, 1, -1)
  ffn_down = ffn_w_down.astype(BF16)

  expand = (jnp.arange(d_inner)[None, :] // SSM_HEAD_DIM
            == jnp.arange(LANES)[:, None]).astype(BF16)

  kv = None
  for layer in range(depth):
    attn = wo = None
    if layer < n_a:
      w_in = jnp.pad(ssd_w_in[layer].astype(BF16),
                     ((0, 0), (0, LANES - n_heads)))
      z, xbc, dt, bt, dtt = _ssd_in(
          h, _row(ssd_norm[layer]), w_in, ssd_conv_w[layer].astype(F32),
          _row(ssd_conv_b[layer]), _row(ssd_dt_bias[layer], LANES),
          pf=pf, n_heads=n_heads, n_z=d_inner, **dims)
      h = _ssd(xbc, bt, z, dt, dtt, h, _row(ssd_a_log[layer], LANES),
               ssd_a_log[layer].astype(F32).reshape(n_heads, 1),
               _row(jnp.repeat(ssd_d_skip[layer], SSM_HEAD_DIM)),
               _row(ssd_gate_norm[layer]), ssd_w_out[layer].astype(BF16),
               expand, **dims)
    else:
      j = layer - n_a
      if layer == n_a:
        kv, q = _qkv(h, _row(kv_norm), _row(sb_norm[j]), w_kv.astype(BF16),
                     sb_w_q[j].astype(BF16), q_scale=SB_HEAD_DIM ** -0.5 * LOG2_E)
      else:
        _, q = _qkv(h, _row(kv_norm), _row(sb_norm[j]), w_kv.astype(BF16),
                    sb_w_q[j].astype(BF16), q_scale=SB_HEAD_DIM ** -0.5 * LOG2_E)
      attn = _attention(q, kv, pf=pf, **dims)
      wo = sb_w_o[j].astype(BF16)
    fin_g = _row(final_norm) if layer == depth - 1 else None
    h = _ffn(h, attn, wo, ffn_g, ffn_up, ffn_cw, ffn_cb, ffn_down, fin_g,
             layer=layer, pf=pf, drop_rows=pf + n_meta, **dims)
  return h.reshape(batch, seq, d).astype(x.dtype)
```

```python
import functools

import jax
import jax.numpy as jnp
from jax import lax
from jax.experimental import pallas as pl
from jax.experimental.pallas import tpu as pltpu

F32 = jnp.float32
BF16 = jnp.bfloat16

N_META = 16
NORM_EPS = 1e-6
SSM_HEAD_DIM = 64
SSM_GROUPS = 4
SSM_STATE = 128
SSM_CHUNK = 256
SB_HEAD_DIM = 64
LOG2_E = 1.4426950408889634

LANES = 128
SUBLANES = 8
IN_ROW_TILE = 256
FFN_ROW_TILE = 256
QKV_ROW_TILE = 768
FFN_COL_TILE = 256
XBC_COL_TILE = 256
STAGE_SLOTS = 2
ATTN_TILE = 256
ATTN_PAIRS = 4
ATTN_DEAD_LOG2 = -160.0
VMEM_LIMIT = 56 * 1024 * 1024


def _resident(shape):
  nd = len(shape)
  return pl.BlockSpec(shape, lambda *_: (0,) * nd, pipeline_mode=pl.Buffered(1))


def _layer_resident(shape, layer):
  nd = len(shape)
  return pl.BlockSpec((None,) + tuple(shape[1:]),
                      lambda *_: (layer,) + (0,) * (nd - 1),
                      pipeline_mode=pl.Buffered(1))


def _stream_specs(stream, tm, nt):
  if isinstance(stream, tuple):
    head, body = stream
    assert head.shape[0] == tm
    return [_resident(head.shape),
            pl.BlockSpec((tm, body.shape[1]), lambda b, i: (
                b * (nt - 1) + jnp.maximum(i - 1, 0), 0))]
  return [pl.BlockSpec((tm, stream.shape[1]), lambda b, i: (b * nt + i, 0))]


def _stream_args(stream):
  return list(stream) if isinstance(stream, tuple) else [stream]


def _stream_tile(refs, i):
  if len(refs) == 2:
    return jnp.where(i == 0, refs[0][...], refs[1][...])
  return refs[0][...]


def _dot(a, b):
  return jnp.dot(a, b, preferred_element_type=F32)


def _split3(x):
  hi = x.astype(BF16)
  r1 = x - hi.astype(F32)
  mid = r1.astype(BF16)
  lo = (r1 - mid.astype(F32)).astype(BF16)
  return hi, mid, lo


def _dot_exact_rhs(x, m):
  hi, mid, lo = _split3(x)
  return _dot(hi, m) + _dot(mid, m) + _dot(lo, m)


def _dot_exact_lhs(m, x):
  hi, mid, lo = _split3(x)
  return _dot(m, hi) + _dot(m, mid) + _dot(m, lo)


def _rmsnorm(x, g):
  ms = jnp.mean(x * x, axis=-1, keepdims=True)
  return x * lax.rsqrt(ms + NORM_EPS) * g


def _sigmoid(x):
  return 1.0 / (1.0 + jnp.exp(-x))


def _softplus(x):
  return jnp.maximum(x, 0.0) + jnp.log(1.0 + jnp.exp(-jnp.abs(x)))


def _causal_conv(p, prev, w, b, stage_ref):
  tm = p.shape[0]
  width = w.shape[0]
  stage_ref[0:SUBLANES, :] = prev
  stage_ref[SUBLANES:SUBLANES + tm, :] = p
  y = p * w[width - 1:width, :] + b
  for k in range(1, width):
    y = y + (stage_ref[SUBLANES - k:SUBLANES - k + tm, :]
             * w[width - 1 - k:width - k, :])
  return y


def _ssd_in_kernel(*refs, pf, n_heads, b_off, n_stream):
  h_refs, refs = refs[:n_stream], refs[n_stream:]
  (g_ref, w_ref, cw_ref, cb_ref, dtb_ref, z_ref, xbc_ref, dt_ref, bt_ref,
   dtt_ref, carry_ref, stage_ref) = refs
  i = pl.program_id(1)
  tm = z_ref.shape[0]
  x_off = z_ref.shape[1]
  dt_off = x_off + xbc_ref.shape[1]
  n_b = bt_ref.shape[0]
  assert b_off % XBC_COL_TILE == 0 and n_b % XBC_COL_TILE == 0

  @pl.when(i == 0)
  def _():
    carry_ref[...] = jnp.zeros_like(carry_ref)

  u = _rmsnorm(_stream_tile(h_refs, i), g_ref[...]).astype(BF16)

  dt = _softplus(_dot(u, w_ref[:, dt_off:dt_off + LANES]) + dtb_ref[...])
  row = i * tm + lax.broadcasted_iota(jnp.int32, dt.shape, 0)
  lane = lax.broadcasted_iota(jnp.int32, dt.shape, 1)
  dt = jnp.where((row >= pf) & (lane < n_heads), dt, 0.0)
  dt_ref[...] = dt
  dtt_ref[...] = dt.T[0:dtt_ref.shape[0], :]

  n_steps = xbc_ref.shape[1] // XBC_COL_TILE
  cols = lambda c: slice(c * XBC_COL_TILE, (c + 1) * XBC_COL_TILE)
  n_z = z_ref.shape[1]
  assert n_z % XBC_COL_TILE == 0 and n_z // XBC_COL_TILE <= n_steps
  xcols = lambda c: slice(x_off + c * XBC_COL_TILE,
                          x_off + (c + 1) * XBC_COL_TILE)
  nxt = _dot(u, w_ref[:, xcols(0)])
  for c in range(n_steps):
    cs = cols(c)
    p = nxt
    if c + 1 < n_steps:
      nxt = _dot(u, w_ref[:, xcols(c + 1)])
    if c < n_z // XBC_COL_TILE:
      z_ref[:, cs] = _dot(u, w_ref[:, cs]).astype(BF16)
    y = _causal_conv(p, carry_ref[:, cs], cw_ref[:, cs], cb_ref[:, cs],
                     stage_ref.at[c % STAGE_SLOTS])
    carry_ref[:, cs] = p[tm - SUBLANES:tm, :]
    act = y * _sigmoid(y)
    xbc_ref[:, cs] = act.astype(BF16)
    if b_off <= cs.start < b_off + n_b:
      bt_ref[cs.start - b_off:cs.stop - b_off, :] = act.T.astype(BF16)


def _ssd_in(h, g, w, cw, cb, dtb, *, batch, lp, pf, n_heads, n_z):
  n_x, n_dt = cw.shape[1], LANES
  assert w.shape[1] == n_z + n_x + n_dt
  tm = IN_ROW_TILE
  nt = lp // tm
  rows = lambda n: pl.BlockSpec((tm, n), lambda b, i: (b * nt + i, 0))
  cols = lambda n: pl.BlockSpec((None, n, tm), lambda b, i: (b, 0, i))
  n_b = SSM_GROUPS * SSM_STATE
  return pl.pallas_call(
      functools.partial(_ssd_in_kernel, pf=pf, n_heads=n_heads,
                        b_off=n_x - 2 * n_b, n_stream=len(_stream_args(h))),
      grid=(batch, nt),
      in_specs=_stream_specs(h, tm, nt) + [
          _resident(g.shape), _resident(w.shape), _resident(cw.shape),
          _resident(cb.shape), _resident(dtb.shape)],
      out_specs=[rows(n_z), rows(n_x), rows(n_dt), cols(n_b), cols(n_heads)],
      out_shape=[jax.ShapeDtypeStruct((batch * lp, n_z), BF16),
                 jax.ShapeDtypeStruct((batch * lp, n_x), BF16),
                 jax.ShapeDtypeStruct((batch * lp, n_dt), F32),
                 jax.ShapeDtypeStruct((batch, n_b, lp), BF16),
                 jax.ShapeDtypeStruct((batch, n_heads, lp), F32)],
      scratch_shapes=[
          pltpu.VMEM((SUBLANES, n_x), F32),
          pltpu.VMEM((STAGE_SLOTS, tm + SUBLANES, XBC_COL_TILE), F32)],
      compiler_params=pltpu.CompilerParams(
          dimension_semantics=("arbitrary", "arbitrary"),
          vmem_limit_bytes=VMEM_LIMIT),
      name="ssd_in",
  )(*_stream_args(h), g, w, cw, cb, dtb)


def _ssd_kernel(*refs, d_inner, n_stream):
  h_refs, refs = refs[:n_stream], refs[n_stream:]
  (xbc_ref, bt_ref, z_ref, dt_ref, dtt_ref, alog_row_ref, alog_col_ref,
   dskip_ref, gg_ref, wout_ref, expand_ref, o_ref, state_ref) = refs
  c = pl.program_id(1)
  q = o_ref.shape[0]
  n_state = SSM_STATE
  g_width = d_inner // SSM_GROUPS
  pairs_per_group = g_width // LANES

  @pl.when(c == 0)
  def _():
    state_ref[...] = jnp.zeros_like(state_ref)

  ri = lax.broadcasted_iota(jnp.int32, (q, q), 0)
  ci = lax.broadcasted_iota(jnp.int32, (q, q), 1)
  causal = ri >= ci
  lower = causal.astype(BF16)
  upper = (ri <= ci).astype(BF16)

  dt = dt_ref[...]
  acs = _dot_exact_lhs(lower, dt * -jnp.exp(alog_row_ref[...]))
  acs_t = _dot_exact_rhs(dtt_ref[...] * -jnp.exp(alog_col_ref[...]), upper)
  a_last = acs[q - 1:q, :]
  decay_to_end = jnp.exp(a_last - acs)
  decay_from_start = jnp.exp(acs)

  dt_b = dt.astype(BF16)
  dte_b = decay_to_end.astype(BF16)
  dfs_b = decay_from_start.astype(BF16)
  last_rows = decay_from_start[q - SUBLANES:q, :]
  even_head = (lax.broadcasted_iota(jnp.int32, (1, g_width), 1) % LANES
               ) < SSM_HEAD_DIM

  c_off = d_inner + SSM_GROUPS * n_state
  out = _stream_tile(h_refs, c)
  pending = None
  for g in range(SSM_GROUPS):
    gs = slice(g * g_width, (g + 1) * g_width)
    expand = expand_ref[:, gs]
    dt_x = _dot(dt_b, expand)
    dte_x = _dot(dte_b, expand)
    dfs_x = _dot(dfs_b, expand)
    chunk_decay_x = _dot_exact_rhs(last_rows, expand)[SUBLANES - 1:SUBLANES, :]

    xs = xbc_ref[:, gs].astype(F32)
    xdt = xs * dt_x
    xdt_end = (xdt * dte_x).astype(BF16)
    xdt_b = xdt.astype(BF16)
    zero = jnp.zeros_like(xdt_b)
    xdt_halves = (jnp.where(even_head, xdt_b, zero),
                  jnp.where(even_head, zero, xdt_b))

    c_g = xbc_ref[:, c_off + g * n_state:c_off + (g + 1) * n_state]
    bt_g = bt_ref[g * n_state:(g + 1) * n_state, :]
    cb = _dot(c_g, bt_g)
    s_prev = state_ref[g]
    y_off = _dot(c_g, s_prev.astype(BF16)) * dfs_x
    state_ref[g] = s_prev * chunk_decay_x + _dot(bt_g, xdt_end)
    if pending is not None:
      out = out + _dot(pending[0], wout_ref[pending[1], :])
    pair_out = []
    for pr in range(pairs_per_group):
      cs = slice(pr * LANES, (pr + 1) * LANES)
      acc = None
      for half, xsrc in enumerate(xdt_halves):
        hd = (g * g_width + pr * LANES) // SSM_HEAD_DIM + half
        diff = acs[:, hd:hd + 1] - acs_t[hd:hd + 1, :]
        decay = jnp.exp(jnp.where(causal, diff, -jnp.inf))
        part = _dot((cb * decay).astype(BF16), xsrc[:, cs])
        acc = part if acc is None else acc + part
      pair_out.append(acc)
    y = jnp.concatenate(pair_out, axis=1) + y_off
    zf = z_ref[:, gs].astype(F32)
    hg = (y + xs * dskip_ref[:, gs]) * (zf * _sigmoid(zf))
    pending = (_rmsnorm(hg, gg_ref[:, gs]).astype(BF16), gs)
  o_ref[...] = out + _dot(pending[0], wout_ref[pending[1], :])


def _ssd(xbc, bt, z, dt, dtt, h, alog_row, alog_col, dskip_x, gate_g, wout,
         expand, *, batch, lp):
  d = wout.shape[1]
  d_inner = z.shape[1]
  q = SSM_CHUNK
  nc = lp // q
  n_heads = dtt.shape[1]
  rows = lambda n: pl.BlockSpec((q, n), lambda b, c: (b * nc + c, 0))
  cols = lambda n: pl.BlockSpec((None, n, q), lambda b, c: (b, 0, c))
  return pl.pallas_call(
      functools.partial(_ssd_kernel, d_inner=d_inner,
                        n_stream=len(_stream_args(h))),
      grid=(batch, nc),
      in_specs=_stream_specs(h, q, nc) + [
          rows(xbc.shape[1]), cols(bt.shape[1]), rows(d_inner),
          rows(dt.shape[1]), cols(n_heads),
          _resident(alog_row.shape), _resident(alog_col.shape),
          _resident(dskip_x.shape), _resident(gate_g.shape),
          _resident(wout.shape), _resident(expand.shape)],
      out_specs=rows(d),
      out_shape=jax.ShapeDtypeStruct((batch * lp, d), F32),
      scratch_shapes=[pltpu.VMEM(
          (SSM_GROUPS, SSM_STATE, d_inner // SSM_GROUPS), F32)],
      compiler_params=pltpu.CompilerParams(
          dimension_semantics=("arbitrary", "arbitrary"),
          vmem_limit_bytes=VMEM_LIMIT),
      name="ssd_scan",
  )(*_stream_args(h), xbc, bt, z, dt, dtt, alog_row, alog_col, dskip_x,
    gate_g, wout, expand)


def _ffn_kernel(*refs, pf, d_ff, has_attn, has_final):
  refs = list(refs)
  h_ref = refs.pop(0)
  if has_attn:
    attn_ref, wo_ref = refs.pop(0), refs.pop(0)
  g_ref, wup_ref, cw_ref, cb_ref, wdn_ref = (refs.pop(0) for _ in range(5))
  if has_final:
    fin_ref = refs.pop(0)
  o_ref, carry_ref, stage_ref = refs
  i = pl.program_id(1)
  tm = h_ref.shape[0]

  @pl.when(i == 0)
  def _():
    carry_ref[...] = jnp.zeros_like(carry_ref)

  h = h_ref[...]
  if has_attn:
    h = h + _dot(attn_ref[...], wo_ref[...])
  u = _rmsnorm(h, g_ref[...]).astype(BF16)

  def cols(c):
    return (slice(c * FFN_COL_TILE, (c + 1) * FFN_COL_TILE),
            slice(d_ff + c * FFN_COL_TILE, d_ff + (c + 1) * FFN_COL_TILE))

  def up(c):
    gs, vs = cols(c)
    return _dot(u, wup_ref[:, gs]), _dot(u, wup_ref[:, vs])

  n_steps = d_ff // FFN_COL_TILE
  acc = jnp.zeros(h.shape, F32)
  nxt = up(0)
  act = None
  for c in range(n_steps):
    gs, vs = cols(c)
    pg, pv = nxt
    if c + 1 < n_steps:
      nxt = up(c + 1)
    if act is not None:
      acc = acc + _dot(act, wdn_ref[cols(c - 1)[0], :])
    slot = 2 * (c % STAGE_SLOTS)
    yg = _causal_conv(pg, carry_ref[:, gs], cw_ref[:, gs], cb_ref[:, gs],
                      stage_ref.at[slot])
    yv = _causal_conv(pv, carry_ref[:, vs], cw_ref[:, vs], cb_ref[:, vs],
                      stage_ref.at[slot + 1])
    carry_ref[:, gs] = pg[tm - SUBLANES:tm, :]
    carry_ref[:, vs] = pv[tm - SUBLANES:tm, :]
    act = (yg * _sigmoid(yg) * yv).astype(BF16)
  acc = acc + _dot(act, wdn_ref[cols(n_steps - 1)[0], :])
  out = h + acc
  row = i * tm + lax.broadcasted_iota(jnp.int32, out.shape, 0)
  out = jnp.where(row >= pf, out, 0.0)
  if has_final:
    out = _rmsnorm(out, fin_ref[...])
  o_ref[...] = out


def _ffn(h, attn, wo, g, wup, cw, cb, wdn, fin_g, *, layer, batch, lp, pf,
         drop_rows):
  d = h.shape[1]
  d_ff = wdn.shape[1]
  tm = FFN_ROW_TILE
  nt = lp // tm
  rows = lambda n: pl.BlockSpec((tm, n), lambda b, i: (b * nt + i, 0))
  has_attn = attn is not None
  has_final = fin_g is not None
  args, specs = [h], [rows(d)]
  if has_attn:
    args += [attn, wo]
    specs += [rows(attn.shape[1]), _resident(wo.shape)]
  args += [g, wup, cw, cb, wdn]
  specs += [_layer_resident(a.shape, layer) for a in (g, wup, cw, cb, wdn)]
  if has_final:
    args.append(fin_g)
    specs.append(_resident(fin_g.shape))
    assert drop_rows == tm
    out_rows = lp - drop_rows
    out_spec = pl.BlockSpec(
        (tm, d), lambda b, i: (b * (nt - 1) + jnp.maximum(i - 1, 0), 0))
  else:
    out_rows = lp
    out_spec = rows(d)
  return pl.pallas_call(
      functools.partial(_ffn_kernel, pf=pf, d_ff=d_ff, has_attn=has_attn,
                        has_final=has_final),
      grid=(batch, nt),
      in_specs=specs,
      out_specs=out_spec,
      out_shape=jax.ShapeDtypeStruct((batch * out_rows, d), F32),
      scratch_shapes=[
          pltpu.VMEM((SUBLANES, 2 * d_ff), F32),
          pltpu.VMEM((2 * STAGE_SLOTS, tm + SUBLANES, FFN_COL_TILE), F32)],
      compiler_params=pltpu.CompilerParams(
          dimension_semantics=("arbitrary", "arbitrary"),
          vmem_limit_bytes=VMEM_LIMIT),
      name="conv_ffn",
  )(*args)


def _qkv_kernel(h_ref, gkv_ref, gq_ref, wkv_ref, wq_ref, kv_ref, q_ref, *,
                q_scale):
  h = h_ref[...]
  hn = h * lax.rsqrt(jnp.mean(h * h, axis=-1, keepdims=True) + NORM_EPS)
  kv_ref[...] = _dot((hn * gkv_ref[...]).astype(BF16), wkv_ref[...]).astype(BF16)
  q = _dot((hn * gq_ref[...]).astype(BF16), wq_ref[...])
  q_ref[...] = (q * q_scale).astype(BF16)


def _qkv(h, gkv, gq, wkv, wq, *, q_scale):
  n, d = h.shape
  rows = lambda m: pl.BlockSpec((QKV_ROW_TILE, m), lambda i: (i, 0))
  return pl.pallas_call(
      functools.partial(_qkv_kernel, q_scale=q_scale),
      grid=(n // QKV_ROW_TILE,),
      in_specs=[rows(d), _resident(gkv.shape), _resident(gq.shape),
                _resident(wkv.shape), _resident(wq.shape)],
      out_specs=[rows(wkv.shape[1]), rows(wq.shape[1])],
      out_shape=[jax.ShapeDtypeStruct((n, wkv.shape[1]), BF16),
                 jax.ShapeDtypeStruct((n, wq.shape[1]), BF16)],
      compiler_params=pltpu.CompilerParams(
          dimension_semantics=("arbitrary",), vmem_limit_bytes=VMEM_LIMIT),
      name="qkv_proj",
  )(h, gkv, gq, wkv, wq)


def _attn_kernel(q_ref, k_ref, v_ref, o_ref, acc_ref, r_ref, live_ref, *, pf):
  i = pl.program_id(2)
  t = q_ref.shape[0]
  n_pairs = q_ref.shape[1] // LANES
  first_head = lax.broadcasted_iota(jnp.int32, (1, LANES), 1) < SB_HEAD_DIM

  half = t // 2

  def block(j, masked, carry, row0=0, nr=t, nk=t, k0=0, has_padding=True):
    assert not (masked and k0)
    accs, r_heads = list(carry[:n_pairs]), carry[n_pairs:]
    start = pl.multiple_of(j * t + k0, half)
    q_heads = []
    for p in range(n_pairs):
      q = q_ref[row0:row0 + nr, p * LANES:(p + 1) * LANES]
      zq = jnp.zeros_like(q)
      q_heads += [jnp.where(first_head, q, zq), jnp.where(first_head, zq, q)]
    neg_suffix = jnp.where(
        lax.broadcasted_iota(jnp.int32, (nk, nk), 0)
        >= lax.broadcasted_iota(jnp.int32, (nk, nk), 1), -1.0, 0.0).astype(BF16)
    if masked:
      s_idx = j * t + lax.broadcasted_iota(jnp.int32, (nr, nk), 1)
      t_idx = i * t + row0 + lax.broadcasted_iota(jnp.int32, (nr, nk), 0)
      visible = s_idx < t_idx
      if has_padding:
        visible = visible & (s_idx >= pf)
    logits = []
    for hd, qa in enumerate(q_heads):
      p = hd // 2
      kb = k_ref[pl.ds(start, nk), p * LANES:(p + 1) * LANES]
      logits.append(lax.dot_general(qa, kb, (((1,), (1,)), ((), ())),
                                    preferred_element_type=F32))
    suffix_sums = []
    for y in logits:
      sp = jnp.maximum(y, jnp.log2(1.0 + jnp.exp2(jnp.minimum(y, 126.0))))
      if masked:
        sp = jnp.where(visible, sp, 0.0)
      suffix_sums.append(_dot(sp.astype(BF16), neg_suffix))
    new_r = []
    for hd, (y, incl, ra) in enumerate(zip(logits, suffix_sums, r_heads)):
      p, second = divmod(hd, 2)
      vb = v_ref[pl.ds(start, nk), p * LANES:(p + 1) * LANES]
      zv = jnp.zeros_like(vb)
      va = (jnp.where(first_head, zv, vb) if second
            else jnp.where(first_head, vb, zv))
      arg = y + ra + incl
      if masked:
        arg = jnp.where(visible, arg, -jnp.inf)
      accs[p] = accs[p] + _dot(jnp.exp2(arg.astype(BF16)), va)
      new_r.append(ra + incl[:, 0:1])
    return (*accs, *new_r)

  def stash(cr, row0=0, nr=t):
    for p in range(n_pairs):
      acc_ref[row0:row0 + nr, p * LANES:(p + 1) * LANES] = cr[p]
    for hd in range(2 * n_pairs):
      r_ref[hd, row0:row0 + nr, :] = cr[n_pairs + hd]

  def stashed():
    return (tuple(acc_ref[:, p * LANES:(p + 1) * LANES] for p in range(n_pairs))
            + tuple(r_ref[hd] for hd in range(2 * n_pairs)))

  def diagonal(has_padding):
    zeros = ((jnp.zeros((half, LANES), F32),) * n_pairs
             + (jnp.zeros((half, 1), F32),) * (2 * n_pairs))
    return (block(i, True, zeros, row0=0, nr=half, nk=half,
                  has_padding=has_padding),
            block(i, True, zeros, row0=half, nr=half, nk=t,
                  has_padding=has_padding))

  def max_carried(cr):
    m = cr[n_pairs]
    for r in cr[n_pairs + 1:]:
      m = jnp.maximum(m, r)
    return jnp.max(m)

  n_loop = jnp.maximum(i - 2, 0)

  @pl.when(i >= 2)
  def _():
    upper, lower = diagonal(has_padding=False)
    upper = block(i - 1, False, upper, row0=0, nr=half, nk=t)
    lower = block(i - 1, False, lower, row0=half, nr=half, nk=half, k0=half)
    stash(upper, 0, half)
    stash(lower, half, half)

  @pl.when(i < 2)
  def _():
    upper, lower = diagonal(has_padding=True)
    stash(upper, 0, half)
    stash(lower, half, half)

  def lower_rows():
    return (tuple(acc_ref[half:t, p * LANES:(p + 1) * LANES]
                  for p in range(n_pairs))
            + tuple(r_ref[hd, half:t, :] for hd in range(2 * n_pairs)))

  def rows_max(lo, hi):
    m = r_ref[0, lo:hi, :]
    for hd in range(1, 2 * n_pairs):
      m = jnp.maximum(m, r_ref[hd, lo:hi, :])
    return jnp.max(m)

  max_upper, max_lower = rows_max(0, half), rows_max(half, t)
  live_ref[0] = jnp.maximum(max_upper, max_lower)

  @pl.when((i >= 2) & (max_lower > ATTN_DEAD_LOG2))
  def _():
    out = block(i - 1, False, lower_rows(), row0=half, nr=half, nk=half, k0=0)
    stash(out, half, half)
    live_ref[0] = jnp.maximum(max_upper, max_carried(out))

  def live(state):
    return (state[0] < n_loop) & (state[1] > ATTN_DEAD_LOG2)

  def step(state):
    out = block(i - 2 - state[0], False, state[2:])
    return (state[0] + 1, max_carried(out), *out)

  @pl.when((n_loop > 0) & (live_ref[0] > ATTN_DEAD_LOG2))
  def _():
    state = lax.while_loop(live, step, (jnp.int32(0), live_ref[0], *stashed()))
    stash(state[2:])
    live_ref[0] = state[1]

  @pl.when((i >= 1) & (live_ref[0] > ATTN_DEAD_LOG2))
  def _():
    stash(block(0, True, stashed()))

  o_ref[...] = acc_ref[...].astype(BF16)


def _attention(q, kv, *, batch, lp, pf):
  n, width = q.shape
  t = ATTN_TILE
  nq = lp // t
  bw = ATTN_PAIRS * LANES
  n_col = width // bw
  return pl.pallas_call(
      functools.partial(_attn_kernel, pf=pf),
      grid=(batch, n_col, nq),
      in_specs=[
          pl.BlockSpec((t, bw), lambda b, p, i: (b * nq + i, p)),
          pl.BlockSpec((lp, bw), lambda b, p, i: (b, p)),
          pl.BlockSpec((lp, bw), lambda b, p, i: (b, n_col + p)),
      ],
      out_specs=pl.BlockSpec((t, bw), lambda b, p, i: (b * nq + i, p)),
      out_shape=jax.ShapeDtypeStruct((n, width), BF16),
      scratch_shapes=[pltpu.VMEM((t, bw), F32),
                      pltpu.VMEM((2 * ATTN_PAIRS, t, 1), F32),
                      pltpu.SMEM((1,), F32)],
      compiler_params=pltpu.CompilerParams(
          dimension_semantics=("arbitrary", "arbitrary", "arbitrary"),
          vmem_limit_bytes=VMEM_LIMIT),
      name="sb_attention",
  )(q, kv, kv)


def _row(v, width=None):
  v = v.astype(F32).reshape(1, -1)
  if width is not None and v.shape[1] < width:
    v = jnp.pad(v, ((0, 0), (0, width - v.shape[1])))
  return v


def kernel(x, meta_tokens, ssd_norm, ssd_w_in, ssd_conv_w, ssd_conv_b, ssd_dt_bias, ssd_a_log, ssd_d_skip, ssd_gate_norm, ssd_w_out, kv_norm, w_kv, sb_norm, sb_w_q, sb_w_o, ffn_norm, ffn_w_up, ffn_conv_w, ffn_conv_b, ffn_w_down, final_norm):
  batch, seq, d = x.shape
  depth = ffn_norm.shape[0]
  n_a = ssd_norm.shape[0]
  d_inner = ssd_w_out.shape[1]
  n_heads = ssd_a_log.shape[1]
  n_meta = meta_tokens.shape[0]
  length = n_meta + seq
  pf = (-n_meta) % SSM_CHUNK
  lp = pf + length
  assert all(lp % t == 0 for t in (SSM_CHUNK, IN_ROW_TILE, FFN_ROW_TILE,
                                   QKV_ROW_TILE, ATTN_TILE))
  assert n_heads <= LANES and d_inner == n_heads * SSM_HEAD_DIM
  assert pf <= ATTN_TILE
  dims = dict(batch=batch, lp=lp)

  head = jnp.concatenate([jnp.zeros((pf, d), F32), meta_tokens.astype(F32)])
  h = (head, x.astype(F32).reshape(batch * seq, d))
  assert n_a >= 1 and depth > n_a

  ffn_g = ffn_norm.astype(F32).reshape(depth, 1, d)
  ffn_up = ffn_w_up.astype(BF16)
  ffn_cw = ffn_conv_w.astype(F32)
  ffn_cb = ffn_conv_b.astype(F32).reshape(depth, 1, -1)
  ffn_down = ffn_w_down.astype(BF16)

  expand = (jnp.arange(d_inner)[None, :] // SSM_HEAD_DIM
            == jnp.arange(LANES)[:, None]).astype(BF16)

  kv = None
  for layer in range(depth):
    attn = wo = None
    if layer < n_a:
      w_in = jnp.pad(ssd_w_in[layer].astype(BF16),
                     ((0, 0), (0, LANES - n_heads)))
      z, xbc, dt, bt, dtt = _ssd_in(
          h, _row(ssd_norm[layer]), w_in, ssd_conv_w[layer].astype(F32),
          _row(ssd_conv_b[layer]), _row(ssd_dt_bias[layer], LANES),
          pf=pf, n_heads=n_heads, n_z=d_inner, **dims)
      h = _ssd(xbc, bt, z, dt, dtt, h, _row(ssd_a_log[layer], LANES),
               ssd_a_log[layer].astype(F32).reshape(n_heads, 1),
               _row(jnp.repeat(ssd_d_skip[layer], SSM_HEAD_DIM)),
               _row(ssd_gate_norm[layer]), ssd_w_out[layer].astype(BF16),
               expand, **dims)
    else:
      j = layer - n_a
      if layer == n_a:
        kv, q = _qkv(h, _row(kv_norm), _row(sb_norm[j]), w_kv.astype(BF16),
                     sb_w_q[j].astype(BF16), q_scale=SB_HEAD_DIM ** -0.5 * LOG2_E)
      else:
        _, q = _qkv(h, _row(kv_norm), _row(sb_norm[j]), w_kv.astype(BF16),
                    sb_w_q[j].astype(BF16), q_scale=SB_HEAD_DIM ** -0.5 * LOG2_E)
      attn = _attention(q, kv, pf=pf, **dims)
      wo = sb_w_o[j].astype(BF16)
    fin_g = _row(final_norm) if layer == depth - 1 else None
    h = _ffn(h, attn, wo, ffn_g, ffn_up, ffn_cw, ffn_cb, ffn_down, fin_g,
             layer=layer, pf=pf, drop_rows=pf + n_meta, **dims)
  return h.reshape(batch, seq, d).astype(x.dtype)
```
